```python
import math
import jax
import jax.numpy as jnp
from jax import lax


D_MODEL = 1024
BATCH = 16
SEQ = 2048
DEPTH = 2

MIX_WIDTH = D_MODEL
N_MIX_GROUPS = 4
GROUP_WIDTH = MIX_WIDTH // N_MIX_GROUPS
HEAD_DIM = 64
N_HEADS_G = GROUP_WIDTH // HEAD_DIM
NORM_EPS = 1e-6
NEG_INF = -1e30

ROPE_THETA = 500000.0
ROPE_DIMS = HEAD_DIM // 4

RET_THETA = 10000.0
RET_CHUNK = 128

MOBA_BLOCK = 256
MOBA_TOPK = 3

SSM_D_INNER = GROUP_WIDTH
SSM_HEADDIM = 64
SSM_HEADS = SSM_D_INNER // SSM_HEADDIM
SSM_NGROUPS = 2
SSM_D_STATE = 128
SSM_CONV = 4
SSM_CHUNK = 256
SSM_CONV_CH = SSM_D_INNER + 2 * SSM_NGROUPS * SSM_D_STATE

NSA_CMP_BLOCK = 32
NSA_CMP_STRIDE = 16
NSA_CMP_HIDDEN = 256
NSA_SLC_BLOCK = 64
NSA_SLC_TOPK = 16
NSA_WINDOW = 512
NSA_FORCE_BONUS = 1e6

PEER_HEADS = 8
PEER_NKEYS = 128
PEER_EXPERTS = PEER_NKEYS * PEER_NKEYS
PEER_KEY_DIM = 128
PEER_TOPK = 16
PEER_TOKEN_BLOCK = 128

Q_BLOCK = 128
SEQ_MULTIPLE = 256

IN_SPLITS = (
    ('ret_q', GROUP_WIDTH), ('ret_k', GROUP_WIDTH), ('ret_v', GROUP_WIDTH), ('ret_g', GROUP_WIDTH),
    ('moba_q', GROUP_WIDTH), ('moba_k', GROUP_WIDTH), ('moba_v', GROUP_WIDTH),
    ('ssm_z', SSM_D_INNER), ('ssm_xbc', SSM_CONV_CH), ('ssm_dt', SSM_HEADS),
    ('nsa_q', GROUP_WIDTH),
    ('nsa_k_cmp', HEAD_DIM), ('nsa_v_cmp', HEAD_DIM),
    ('nsa_k_slc', HEAD_DIM), ('nsa_v_slc', HEAD_DIM),
    ('nsa_k_win', HEAD_DIM), ('nsa_v_win', HEAD_DIM),
    ('nsa_gate', 3 * N_HEADS_G),
)
IN_COLS = sum(w for _, w in IN_SPLITS)

kernel_name = 'hybrid_ret_moba_ssd_nsa_peer'


def rmsnorm(x, w):
    x32 = x.astype(jnp.float32)
    y = x32 * lax.rsqrt(jnp.mean(x32 * x32, axis=-1, keepdims=True) + NORM_EPS)
    return (y * w.astype(jnp.float32)).astype(x.dtype)


def split_columns(y):
    parts = {}
    off = 0
    for name, width in IN_SPLITS:
        parts[name] = y[..., off:off + width]
        off += width
    return parts


def rope_tables(seq, rot_dims, theta):
    inv = 1.0 / (theta ** (jnp.arange(0, rot_dims, 2, dtype=jnp.float32) / rot_dims))
    ang = jnp.arange(seq, dtype=jnp.float32)[:, None] * inv[None, :]
    return jnp.cos(ang), jnp.sin(ang)


def apply_rope(x, cos, sin):
    half = cos.shape[-1]
    rot = 2 * half
    c = cos[None, :, None, :].astype(x.dtype)
    s = sin[None, :, None, :].astype(x.dtype)
    x1 = x[..., :half]
    x2 = x[..., half:rot]
    return jnp.concatenate([x1 * c - x2 * s, x1 * s + x2 * c, x[..., rot:]], axis=-1)


def retention(q, k, v, g, norm_w):
    b, s, h, d = q.shape
    cos, sin = rope_tables(s, d, RET_THETA)
    q = apply_rope(q, cos, sin)
    k = apply_rope(k, cos, sin) * (d ** -0.5)
    c = RET_CHUNK
    n = s // c
    log_gamma = jnp.log1p(-jnp.exp2(-5.0 - jnp.arange(h, dtype=jnp.float32)))
    pos = jnp.arange(c, dtype=jnp.float32)
    diff = pos[:, None] - pos[None, :]
    intra = jnp.where(diff >= 0, jnp.exp(log_gamma[:, None, None] * jnp.maximum(diff, 0.0)), 0.0).astype(q.dtype)
    q_dec = jnp.exp(log_gamma[:, None] * (pos + 1.0)).astype(q.dtype)
    k_dec = jnp.exp(log_gamma[:, None] * (c - 1.0 - pos)).astype(q.dtype)
    c_dec = jnp.exp(log_gamma * c).astype(q.dtype)
    qc = q.reshape(b, n, c, h, d)
    kc = k.reshape(b, n, c, h, d)
    vc = v.reshape(b, n, c, h, d)
    att = jnp.einsum('bnihd,bnjhd->bnhij', qc, kc) * intra
    o_in = jnp.einsum('bnhij,bnjhe->bnihe', att, vc)
    kv = jnp.einsum('bnjhd,hj,bnjhe->nbhde', kc, k_dec, vc)

    def step(state, kv_n):
        return state * c_dec[None, :, None, None] + kv_n, state

    _, prev = lax.scan(step, jnp.zeros_like(kv[0]), kv)
    o_x = jnp.einsum('bnihd,nbhde,hi->bnihe', qc, prev, q_dec)
    o = (o_in + o_x).reshape(b, s, h, d)
    o = rmsnorm(o, norm_w.reshape(h, d))
    return jax.nn.silu(g) * o.reshape(b, s, h * d)


def moba(q, k, v):
    b, s, h, d = q.shape
    nb = s // MOBA_BLOCK
    topk = min(MOBA_TOPK, nb)
    scale = d ** -0.5
    qh = q.transpose(0, 2, 1, 3)
    kblk = k.transpose(0, 2, 1, 3).reshape(b, h, nb, MOBA_BLOCK, d)
    vblk = v.transpose(0, 2, 1, 3).reshape(b, h, nb, MOBA_BLOCK, d)
    kmean = jnp.mean(kblk, axis=3)
    blk_pos = jnp.arange(MOBA_BLOCK)
    head_ix = jnp.arange(h)[:, None, None]

    def one_batch(args):
        qb, kb, vb, kmb = args

        def one_chunk(ci):
            t0 = ci * Q_BLOCK
            tpos = t0 + jnp.arange(Q_BLOCK)
            cur = t0 // MOBA_BLOCK
            qc = lax.dynamic_slice_in_dim(qb, t0, Q_BLOCK, axis=1)
            gate = jnp.einsum('hqd,hnd->hqn', qc, kmb).astype(jnp.float32)
            gate = jnp.where(jnp.arange(nb) < cur, gate, NEG_INF)
            _, sel = lax.top_k(gate, topk)
            sel_ok = sel < cur
            kg = kb[head_ix, sel]
            vg = vb[head_ix, sel]
            s_sel = jnp.einsum('hqd,hqnjd->hqnj', qc, kg).astype(jnp.float32) * scale
            s_sel = jnp.where(sel_ok[..., None], s_sel, NEG_INF).reshape(h, Q_BLOCK, topk * MOBA_BLOCK)
            k_own = lax.dynamic_index_in_dim(kb, cur, axis=1, keepdims=False)
            v_own = lax.dynamic_index_in_dim(vb, cur, axis=1, keepdims=False)
            s_own = jnp.einsum('hqd,hjd->hqj', qc, k_own).astype(jnp.float32) * scale
            own_ok = (cur * MOBA_BLOCK + blk_pos)[None, :] <= tpos[:, None]
            s_own = jnp.where(own_ok, s_own, NEG_INF)
            p = jax.nn.softmax(jnp.concatenate([s_sel, s_own], axis=-1), axis=-1).astype(qc.dtype)
            p_sel = p[..., :topk * MOBA_BLOCK].reshape(h, Q_BLOCK, topk, MOBA_BLOCK)
            p_own = p[..., topk * MOBA_BLOCK:]
            return (jnp.einsum('hqnj,hqnjd->qhd', p_sel, vg)
                    + jnp.einsum('hqj,hjd->qhd', p_own, v_own))

        out = lax.map(one_chunk, jnp.arange(s // Q_BLOCK))
        return out.reshape(s, h, d)

    return lax.map(one_batch, (qh, kblk, vblk, kmean))


def ssd_chunked(x, dt, a, bm, cm):
    b, s, h, p = x.shape
    g, n = bm.shape[2], bm.shape[3]
    rep = h // g
    L = SSM_CHUNK
    nc = s // L
    bh = jnp.repeat(bm, rep, axis=2).reshape(b, nc, L, h, n)
    ch = jnp.repeat(cm, rep, axis=2).reshape(b, nc, L, h, n)
    xdt = (x * dt[..., None].astype(x.dtype)).reshape(b, nc, L, h, p)
    acum = jnp.cumsum((dt * a).reshape(b, nc, L, h), axis=2)
    causal = jnp.tril(jnp.ones((L, L), dtype=bool))[None, None, :, :, None]
    seg = acum[:, :, :, None, :] - acum[:, :, None, :, :]
    decay = jnp.where(causal, jnp.exp(jnp.where(causal, seg, 0.0)), 0.0).astype(x.dtype)
    scores = jnp.einsum('bclhn,bcshn->bclsh', ch, bh) * decay
    y_diag = jnp.einsum('bclsh,bcshp->bclhp', scores, xdt)
    to_end = jnp.exp(acum[:, :, -1:, :] - acum).astype(x.dtype)
    states = jnp.einsum('bclhn,bclh,bclhp->cbhpn', bh, to_end, xdt)
    chunk_dec = jnp.exp(acum[:, :, -1, :]).astype(x.dtype).transpose(1, 0, 2)

    def step(st, inp):
        s_c, d_c = inp
        return st * d_c[:, :, None, None] + s_c, st

    _, prev = lax.scan(step, jnp.zeros_like(states[0]), (states, chunk_dec))
    y_off = jnp.einsum('bclhn,cbhpn,bclh->bclhp', ch, prev, jnp.exp(acum).astype(x.dtype))
    return (y_diag + y_off).reshape(b, s, h, p)


def mamba2(z, xbc, dt_raw, conv_w, conv_b, dt_bias, a_log, d_skip, norm_w):
    b, s, _ = xbc.shape
    gn = SSM_NGROUPS * SSM_D_STATE
    xbc = lax.conv_general_dilated(xbc, conv_w[:, None, :], window_strides=(1,),
                                   padding=[(SSM_CONV - 1, 0)],
                                   dimension_numbers=('NWC', 'WIO', 'NWC'),
                                   feature_group_count=SSM_CONV_CH) + conv_b
    xbc = jax.nn.silu(xbc)
    xs = xbc[..., :SSM_D_INNER].reshape(b, s, SSM_HEADS, SSM_HEADDIM)
    bm = xbc[..., SSM_D_INNER:SSM_D_INNER + gn].reshape(b, s, SSM_NGROUPS, SSM_D_STATE)
    cm = xbc[..., SSM_D_INNER + gn:].reshape(b, s, SSM_NGROUPS, SSM_D_STATE)
    dt = jax.nn.softplus((dt_raw + dt_bias).astype(jnp.float32))
    a = -jnp.exp(a_log.astype(jnp.float32))
    y = ssd_chunked(xs, dt, a, bm, cm) + xs * d_skip[:, None]
    y = y.reshape(b, s, SSM_D_INNER) * jax.nn.silu(z)
    y = rmsnorm(y.reshape(b, s, SSM_NGROUPS, SSM_D_INNER // SSM_NGROUPS),
                norm_w.reshape(SSM_NGROUPS, SSM_D_INNER // SSM_NGROUPS))
    return y.reshape(b, s, SSM_D_INNER)


def nsa(q, k_cmp, v_cmp, k_slc, v_slc, k_win, v_win, gate_raw,
        pe_k, w1_k, w2_k, pe_v, w1_v, w2_v):
    b, s, h, d = q.shape
    scale = d ** -0.5
    tpos = jnp.arange(s)
    nc = (s - NSA_CMP_BLOCK) // NSA_CMP_STRIDE + 1
    starts = jnp.arange(nc) * NSA_CMP_STRIDE
    win_idx = starts[:, None] + jnp.arange(NSA_CMP_BLOCK)[None, :]

    def compress(t, pe, w1, w2):
        blocks = (t[:, win_idx] + pe).reshape(b, nc, NSA_CMP_BLOCK * d)
        return jax.nn.gelu(blocks @ w1, approximate=False) @ w2

    kc = compress(k_cmp, pe_k, w1_k, w2_k)
    vc = compress(v_cmp, pe_v, w1_v, w2_v)
    cmp_ok = (starts + NSA_CMP_BLOCK - 1)[None, :] <= tpos[:, None]
    s_cmp = jnp.einsum('bshd,bcd->bhsc', q, kc).astype(jnp.float32) * scale
    s_cmp = jnp.where(cmp_ok, s_cmp, NEG_INF)
    any_ok = jnp.any(cmp_ok, axis=-1, keepdims=True).astype(jnp.float32)
    p_cmp = jax.nn.softmax(s_cmp, axis=-1) * any_ok
    o_cmp = jnp.einsum('bhsc,bcd->bshd', p_cmp.astype(q.dtype), vc)
    nsb = s // NSA_SLC_BLOCK
    sb = jnp.arange(nsb)
    overlap = ((starts[:, None] < (sb[None, :] + 1) * NSA_SLC_BLOCK)
               & ((starts + NSA_CMP_BLOCK)[:, None] > sb[None, :] * NSA_SLC_BLOCK)).astype(jnp.float32)
    imp = jnp.einsum('bhsc,cj->bsj', p_cmp, overlap)
    cur_b = tpos // NSA_SLC_BLOCK
    slc_ok = sb[None, :] <= cur_b[:, None]
    forced = (sb[None, :] == 0) | (sb[None, :] == cur_b[:, None]) | (sb[None, :] == cur_b[:, None] - 1)
    imp = jnp.where(forced, imp + NSA_FORCE_BONUS, imp)
    imp = jnp.where(slc_ok, imp, NEG_INF)
    ksel = min(NSA_SLC_TOPK, nsb)
    _, sel = lax.top_k(imp, ksel)
    ks_blk = k_slc.reshape(b, nsb, NSA_SLC_BLOCK, d)
    vs_blk = v_slc.reshape(b, nsb, NSA_SLC_BLOCK, d)
    kw = jnp.pad(k_win, ((0, 0), (NSA_WINDOW, 0), (0, 0)))
    vw = jnp.pad(v_win, ((0, 0), (NSA_WINDOW, 0), (0, 0)))
    slc_pos = jnp.arange(NSA_SLC_BLOCK)
    win_span = jnp.arange(NSA_WINDOW + Q_BLOCK)

    def one_batch(args):
        qb, ksb, vsb, kwb, vwb, selb = args

        def one_chunk(ci):
            t0 = ci * Q_BLOCK
            tq = t0 + jnp.arange(Q_BLOCK)
            qc = lax.dynamic_slice_in_dim(qb, t0, Q_BLOCK, axis=0)
            sc = lax.dynamic_slice_in_dim(selb, t0, Q_BLOCK, axis=0)
            kg = ksb[sc]
            vg = vsb[sc]
            key_pos = sc[..., None] * NSA_SLC_BLOCK + slc_pos
            ok = key_pos <= tq[:, None, None]
            s_s = jnp.einsum('qhd,qnjd->qhnj', qc, kg).astype(jnp.float32) * scale
            s_s = jnp.where(ok[:, None], s_s, NEG_INF).reshape(Q_BLOCK, h, ksel * NSA_SLC_BLOCK)
            p_s = jax.nn.softmax(s_s, axis=-1).astype(qc.dtype).reshape(Q_BLOCK, h, ksel, NSA_SLC_BLOCK)
            o_s = jnp.einsum('qhnj,qnjd->qhd', p_s, vg)
            kwc = lax.dynamic_slice_in_dim(kwb, t0, NSA_WINDOW + Q_BLOCK, axis=0)
            vwc = lax.dynamic_slice_in_dim(vwb, t0, NSA_WINDOW + Q_BLOCK, axis=0)
            kpos = t0 - NSA_WINDOW + win_span
            wok = ((kpos[None, :] <= tq[:, None]) & (kpos[None, :] > tq[:, None] - NSA_WINDOW)
                   & (kpos[None, :] >= 0))
            s_w = jnp.einsum('qhd,kd->qhk', qc, kwc).astype(jnp.float32) * scale
            s_w = jnp.where(wok[:, None, :], s_w, NEG_INF)
            o_w = jnp.einsum('qhk,kd->qhd', jax.nn.softmax(s_w, axis=-1).astype(qc.dtype), vwc)
            return o_s, o_w

        o_s, o_w = lax.map(one_chunk, jnp.arange(s // Q_BLOCK))
        return o_s.reshape(s, h, d), o_w.reshape(s, h, d)

    o_slc, o_win = lax.map(one_batch, (q, ks_blk, vs_blk, kw, vw, sel))
    gates = jax.nn.sigmoid(gate_raw.astype(jnp.float32)).astype(q.dtype).reshape(b, s, 3, h)
    o = (gates[:, :, 0, :, None] * o_cmp + gates[:, :, 1, :, None] * o_slc
         + gates[:, :, 2, :, None] * o_win)
    return o.reshape(b, s, h * d)


def token_mixer(h, w_in, ret_norm, ssm_conv_w, ssm_conv_b, ssm_dt_bias, ssm_a_log, ssm_d,
                ssm_norm, nsa_pe_k, nsa_w1_k, nsa_w2_k, nsa_pe_v, nsa_w1_v, nsa_w2_v, w_out):
    b, s, _ = h.shape
    s_pad = -(-s // SEQ_MULTIPLE) * SEQ_MULTIPLE
    hp = jnp.pad(h, ((0, 0), (0, s_pad - s), (0, 0)))
    c = split_columns(hp @ w_in)

    def heads(t):
        return t.reshape(b, s_pad, -1, HEAD_DIM)

    cos, sin = rope_tables(s_pad, ROPE_DIMS, ROPE_THETA)

    def rope_kv(t):
        return apply_rope(t[:, :, None, :], cos, sin)[:, :, 0, :]

    o_ret = retention(heads(c['ret_q']), heads(c['ret_k']), heads(c['ret_v']), c['ret_g'], ret_norm)
    o_moba = moba(apply_rope(heads(c['moba_q']), cos, sin),
                  apply_rope(heads(c['moba_k']), cos, sin),
                  heads(c['moba_v'])).reshape(b, s_pad, GROUP_WIDTH)
    o_ssm = mamba2(c['ssm_z'], c['ssm_xbc'], c['ssm_dt'], ssm_conv_w, ssm_conv_b,
                   ssm_dt_bias, ssm_a_log, ssm_d, ssm_norm)
    o_nsa = nsa(apply_rope(heads(c['nsa_q']), cos, sin),
                rope_kv(c['nsa_k_cmp']), c['nsa_v_cmp'],
                rope_kv(c['nsa_k_slc']), c['nsa_v_slc'],
                rope_kv(c['nsa_k_win']), c['nsa_v_win'], c['nsa_gate'],
                nsa_pe_k, nsa_w1_k, nsa_w2_k, nsa_pe_v, nsa_w1_v, nsa_w2_v)
    mixed = jnp.concatenate([o_ret, o_moba, o_ssm, o_nsa], axis=-1)[:, :s]
    return mixed @ w_out


def peer(h, w_q, sub_keys, u, v):
    b, s, dm = h.shape
    tb = PEER_TOKEN_BLOCK
    hb = h.reshape((b * s) // tb, tb, dm)

    def one_block(xb):
        q = (xb @ w_q).reshape(tb, PEER_HEADS, 2, PEER_KEY_DIM)
        sc = jnp.einsum('thpk,hpnk->thpn', q, sub_keys).astype(jnp.float32)
        s1, i1 = lax.top_k(sc[:, :, 0], PEER_TOPK)
        s2, i2 = lax.top_k(sc[:, :, 1], PEER_TOPK)
        cand = (s1[..., :, None] + s2[..., None, :]).reshape(tb, PEER_HEADS, PEER_TOPK * PEER_TOPK)
        cidx = (i1[..., :, None] * PEER_NKEYS + i2[..., None, :]).reshape(tb, PEER_HEADS, PEER_TOPK * PEER_TOPK)
        top_s, pos = lax.top_k(cand, PEER_TOPK)
        idx = jnp.take_along_axis(cidx, pos, axis=-1)
        gw = jax.nn.softmax(top_s, axis=-1).astype(xb.dtype)
        act = jax.nn.gelu(jnp.einsum('td,thkd->thk', xb, u[idx]), approximate=False)
        return jnp.einsum('thk,thkd->td', act * gw, v[idx])

    return lax.map(one_block, hb).reshape(b, s, dm)


def setup_inputs(seed: int = 0) -> dict:
    key = jax.random.key(seed)
    k = jax.random.split(key, 23)
    f32 = jnp.float32
    L = DEPTH

    def normal(kk, shape, scale):
        return scale * jax.random.normal(kk, shape, f32)

    def gain(kk, shape):
        return 1.0 + 0.05 * jax.random.normal(kk, shape, f32)

    flat_cmp = NSA_CMP_BLOCK * HEAD_DIM
    dt0 = jnp.exp(jax.random.uniform(k[7], (L, SSM_HEADS), f32, math.log(1e-3), math.log(1e-1)))
    return {
        'x': normal(k[0], (BATCH, SEQ, D_MODEL), 1.0),
        'norm_mix': gain(k[1], (L, D_MODEL)),
        'w_in': normal(k[2], (L, D_MODEL, IN_COLS), D_MODEL ** -0.5),
        'ret_norm': gain(k[3], (L, GROUP_WIDTH)),
        'ssm_conv_w': normal(k[4], (L, SSM_CONV, SSM_CONV_CH), SSM_CONV ** -0.5),
        'ssm_conv_b': normal(k[5], (L, SSM_CONV_CH), 0.02),
        'ssm_dt_bias': dt0 + jnp.log(-jnp.expm1(-dt0)),
        'ssm_a_log': jnp.log(jax.random.uniform(k[8], (L, SSM_HEADS), f32, 1.0, 16.0)),
        'ssm_d': gain(k[9], (L, SSM_HEADS)),
        'ssm_norm': gain(k[10], (L, SSM_D_INNER)),
        'nsa_pe_k': normal(k[11], (L, NSA_CMP_BLOCK, HEAD_DIM), 0.1),
        'nsa_w1_k': normal(k[12], (L, flat_cmp, NSA_CMP_HIDDEN), flat_cmp ** -0.5),
        'nsa_w2_k': normal(k[13], (L, NSA_CMP_HIDDEN, HEAD_DIM), NSA_CMP_HIDDEN ** -0.5),
        'nsa_pe_v': normal(k[14], (L, NSA_CMP_BLOCK, HEAD_DIM), 0.1),
        'nsa_w1_v': normal(k[15], (L, flat_cmp, NSA_CMP_HIDDEN), flat_cmp ** -0.5),
        'nsa_w2_v': normal(k[16], (L, NSA_CMP_HIDDEN, HEAD_DIM), NSA_CMP_HIDDEN ** -0.5),
        'w_out': normal(k[17], (L, MIX_WIDTH, D_MODEL), MIX_WIDTH ** -0.5),
        'norm_ffn': gain(k[18], (L, D_MODEL)),
        'peer_wq': normal(k[19], (L, D_MODEL, PEER_HEADS * 2 * PEER_KEY_DIM), D_MODEL ** -0.5),
        'peer_keys': normal(k[20], (L, PEER_HEADS, 2, PEER_NKEYS, PEER_KEY_DIM), PEER_KEY_DIM ** -0.5),
        'peer_u': normal(k[21], (L, PEER_EXPERTS, D_MODEL), D_MODEL ** -0.5),
        'peer_v': normal(k[22], (L, PEER_EXPERTS, D_MODEL), PEER_HEADS ** -0.5),
        'norm_final': gain(k[6], (D_MODEL,)),
    }


def reference(x, norm_mix, w_in, ret_norm, ssm_conv_w, ssm_conv_b, ssm_dt_bias, ssm_a_log,
              ssm_d, ssm_norm, nsa_pe_k, nsa_w1_k, nsa_w2_k, nsa_pe_v, nsa_w1_v, nsa_w2_v,
              w_out, norm_ffn, peer_wq, peer_keys, peer_u, peer_v, norm_final):
    for i in range(DEPTH):
        h = rmsnorm(x, norm_mix[i])
        x = x + token_mixer(h, w_in[i], ret_norm[i], ssm_conv_w[i], ssm_conv_b[i],
                            ssm_dt_bias[i], ssm_a_log[i], ssm_d[i], ssm_norm[i],
                            nsa_pe_k[i], nsa_w1_k[i], nsa_w2_k[i],
                            nsa_pe_v[i], nsa_w1_v[i], nsa_w2_v[i], w_out[i])
        x = x + peer(rmsnorm(x, norm_ffn[i]), peer_wq[i], peer_keys[i], peer_u[i], peer_v[i])
    return rmsnorm(x, norm_final)
```

```python
import functools
import math

import numpy as np
import jax
import jax.numpy as jnp
from jax import lax
from jax.experimental import pallas as pl
from jax.experimental.pallas import tpu as pltpu

F32 = jnp.float32
BF16 = jnp.bfloat16

D_MODEL = 1024
SEQ = 2048
GROUP_WIDTH = 256
HEAD_DIM = 64
N_HEADS = 4
NORM_EPS = 1e-6
NEG = -1e30

ROPE_THETA = 500000.0
ROPE_DIMS = 16
RET_THETA = 10000.0
RET_CHUNK = 128
MOBA_BLOCK = 256
MOBA_TOPK = 3
SSM_CHUNK = 256
SSM_STATE = 128
SSM_CONV = 4
NSA_CMP_BLOCK = 32
NSA_CMP_STRIDE = 16
NSA_SLC_BLOCK = 64
NSA_SLC_TOPK = 16
NSA_WINDOW = 512
NSA_FORCE_BONUS = 1e6
PEER_HEADS = 8
PEER_NKEYS = 128
PEER_TOPK = 16
PEER_PICKS = PEER_HEADS * PEER_TOPK
Q_BLOCK = 128

W_RET, W_MOBA, W_SSM, W_NSA, W_SMALL = 1024, 768, 1024, 640, 128
IN_COLS_PADDED = W_RET + W_MOBA + W_SSM + W_NSA + W_SMALL
ROPE_W = 512

IN_TM = 256
PEER_TB = 64
VMEM_LIMIT = 56 * 1024 * 1024


def _params(sem, vmem=VMEM_LIMIT):
    return pltpu.CompilerParams(dimension_semantics=sem, vmem_limit_bytes=vmem)


def _dot(a, b):
    return jnp.dot(a, b, preferred_element_type=F32)


def _dot_nt(a, b):
    return lax.dot_general(a, b, (((1,), (1,)), ((), ())), preferred_element_type=F32)


def _dot_tn(a, b):
    return lax.dot_general(a, b, (((0,), (0,)), ((), ())), preferred_element_type=F32)


def _split_bf16(x):
    hi = x.astype(BF16)
    lo = (x - hi.astype(F32)).astype(BF16)
    return hi, lo


def _silu(x):
    return x * (1.0 / (1.0 + jnp.exp(-x)))


def _gelu(x):
    return 0.5 * x * (1.0 + lax.erf(x * (1.0 / math.sqrt(2.0))))


def _rope_region(seq, theta, rot_dims, n_blocks, scale_blocks=(), ident_blocks=()):
    half = rot_dims // 2
    inv = 1.0 / (theta ** (np.arange(0, rot_dims, 2, dtype=np.float64) / rot_dims))
    ang = np.arange(seq, dtype=np.float64)[:, None] * inv[None, :]
    cos, sin = np.cos(ang), np.sin(ang)
    c = np.ones((seq, HEAD_DIM)); a = np.zeros((seq, HEAD_DIM)); b = np.zeros((seq, HEAD_DIM))
    c[:, :half] = cos; c[:, half:rot_dims] = cos
    a[:, :half] = -sin
    b[:, half:rot_dims] = sin
    cs, as_, bs = [], [], []
    for blk in range(n_blocks):
        if blk in ident_blocks:
            cs.append(np.ones_like(c)); as_.append(np.zeros_like(a)); bs.append(np.zeros_like(b))
            continue
        f = scale_blocks.get(blk, 1.0) if isinstance(scale_blocks, dict) else 1.0
        cs.append(c * f); as_.append(a * f); bs.append(b * f)
    return np.concatenate(cs, 1), np.concatenate(as_, 1), np.concatenate(bs, 1)


@functools.lru_cache(maxsize=None)
def _rope_tables():
    kscale = HEAD_DIM ** -0.5
    ret = _rope_region(SEQ, RET_THETA, HEAD_DIM, 8, scale_blocks={4: kscale, 5: kscale, 6: kscale, 7: kscale})
    moba = _rope_region(SEQ, ROPE_THETA, ROPE_DIMS, 8)
    nsa = _rope_region(SEQ, ROPE_THETA, ROPE_DIMS, 8, ident_blocks=(5,))
    return tuple(np.concatenate([ret[i], moba[i], nsa[i]], 1).astype(np.float32) for i in range(3))


@functools.lru_cache(maxsize=None)
def _ret_tables():
    h, c = N_HEADS, RET_CHUNK
    log_gamma = np.log1p(-np.exp2(-5.0 - np.arange(h, dtype=np.float64)))
    pos = np.arange(c, dtype=np.float64)
    diff = pos[:, None] - pos[None, :]
    intra = np.where(diff >= 0, np.exp(log_gamma[:, None, None] * np.maximum(diff, 0.0)), 0.0)
    q_dec = np.exp(log_gamma[:, None] * (pos + 1.0))
    k_dec = np.exp(log_gamma[:, None] * (c - 1.0 - pos))
    c_dec = np.exp(log_gamma * c)
    qd = np.repeat(q_dec.T, HEAD_DIM, axis=1)
    kd = np.repeat(k_dec.T, HEAD_DIM, axis=1)
    cd = np.broadcast_to(c_dec[:, None, None], (h, HEAD_DIM, HEAD_DIM))
    return (intra.astype(np.float32), qd.astype(np.float32), kd.astype(np.float32),
            np.ascontiguousarray(cd).astype(np.float32))


@functools.lru_cache(maxsize=None)
def _nsa_overlap():
    nc = (SEQ - NSA_CMP_BLOCK) // NSA_CMP_STRIDE + 1
    starts = np.arange(128) * NSA_CMP_STRIDE
    sb = np.arange(128)
    ov = ((starts[:, None] < (sb[None, :] + 1) * NSA_SLC_BLOCK)
          & ((starts + NSA_CMP_BLOCK)[:, None] > sb[None, :] * NSA_SLC_BLOCK))
    ov = ov & (np.arange(128)[:, None] < nc) & (sb[None, :] < SEQ // NSA_SLC_BLOCK)
    return ov.astype(np.float32)


def _inproj_kernel(x_ref, nw_ref, w_ref, rc_ref, ra_ref, rb_ref,
                   o_ret, o_moba, o_ssm, o_nsa, o_small):
    x = x_ref[0]
    ms = jnp.mean(x * x, axis=-1, keepdims=True)
    h = (x * lax.rsqrt(ms + NORM_EPS) * nw_ref[...]).astype(BF16)

    def rope(y, region, shift):
        sl = slice(region * ROPE_W, (region + 1) * ROPE_W)
        return (y * rc_ref[:, sl] + pltpu.roll(y, ROPE_W - shift, 1) * ra_ref[:, sl]
                + pltpu.roll(y, shift, 1) * rb_ref[:, sl])

    off = 0
    y = _dot(h, w_ref[:, off:off + W_RET]); off += W_RET
    o_ret[0, :, :ROPE_W] = rope(y[:, :ROPE_W], 0, HEAD_DIM // 2)
    o_ret[0, :, ROPE_W:] = y[:, ROPE_W:]
    y = _dot(h, w_ref[:, off:off + W_MOBA]); off += W_MOBA
    o_moba[0, :, :ROPE_W] = rope(y[:, :ROPE_W], 1, ROPE_DIMS // 2)
    o_moba[0, :, ROPE_W:] = y[:, ROPE_W:]
    o_ssm[0] = _dot(h, w_ref[:, off:off + W_SSM]); off += W_SSM
    y = _dot(h, w_ref[:, off:off + W_NSA]); off += W_NSA
    o_nsa[0, :, :ROPE_W] = rope(y[:, :ROPE_W], 2, ROPE_DIMS // 2)
    o_nsa[0, :, ROPE_W:] = y[:, ROPE_W:]
    o_small[0] = _dot(h, w_ref[:, off:off + W_SMALL])


def _inproj(x, nw, w, rc, ra, rb):
    b, s, d = x.shape
    ns = s // IN_TM
    widths = (W_RET, W_MOBA, W_SSM, W_NSA, W_SMALL)
    tab = pl.BlockSpec((IN_TM, 3 * ROPE_W), lambda si, bi: (si, 0))
    return pl.pallas_call(
        _inproj_kernel,
        grid=(ns, b),
        in_specs=[pl.BlockSpec((1, IN_TM, d), lambda si, bi: (bi, si, 0)),
                  pl.BlockSpec((1, d), lambda si, bi: (0, 0)),
                  pl.BlockSpec((d, IN_COLS_PADDED), lambda si, bi: (0, 0)),
                  tab, tab, tab],
        out_specs=[pl.BlockSpec((1, IN_TM, wd), lambda si, bi: (bi, si, 0)) for wd in widths],
        out_shape=[jax.ShapeDtypeStruct((b, s, wd), F32) for wd in widths],
        compiler_params=_params(("parallel", "parallel")),
        name="inproj",
    )(x, nw, w, rc, ra, rb)


def _ret_kernel(y_ref, intra_ref, qd_ref, kd_ref, cd_ref, nw_ref, o_ref, state_ref):
    @pl.when(pl.program_id(1) == 0)
    def _():
        state_ref[...] = jnp.zeros_like(state_ref)

    for h in range(N_HEADS):
        sl = slice(h * HEAD_DIM, (h + 1) * HEAD_DIM)
        q = y_ref[0, :, h * HEAD_DIM:(h + 1) * HEAD_DIM].astype(BF16)
        k = y_ref[0, :, GROUP_WIDTH + h * HEAD_DIM:GROUP_WIDTH + (h + 1) * HEAD_DIM]
        v = y_ref[0, :, 2 * GROUP_WIDTH + h * HEAD_DIM:2 * GROUP_WIDTH + (h + 1) * HEAD_DIM].astype(BF16)
        g = y_ref[0, :, 3 * GROUP_WIDTH + h * HEAD_DIM:3 * GROUP_WIDTH + (h + 1) * HEAD_DIM]
        att = _dot_nt(q, k.astype(BF16)) * intra_ref[h]
        o_in = _dot(att.astype(BF16), v)
        st = state_ref[h]
        o_x = _dot(q, st.astype(BF16)) * qd_ref[:, sl]
        kv = _dot_tn((k * kd_ref[:, sl]).astype(BF16), v)
        state_ref[h] = st * cd_ref[h] + kv
        o = o_in + o_x
        o = o * lax.rsqrt(jnp.mean(o * o, axis=-1, keepdims=True) + NORM_EPS) * nw_ref[:, sl]
        o_ref[0, :, sl] = _silu(g) * o


def _retention(y_ret, intra, qd, kd, cd, nw):
    b, s, _ = y_ret.shape
    nch = s // RET_CHUNK
    full = lambda shape: pl.BlockSpec(shape, lambda bi, ci: (0,) * len(shape))
    return pl.pallas_call(
        _ret_kernel,
        grid=(b, nch),
        in_specs=[pl.BlockSpec((1, RET_CHUNK, W_RET), lambda bi, ci: (bi, ci, 0)),
                  full(intra.shape), full(qd.shape), full(kd.shape), full(cd.shape), full(nw.shape)],
        out_specs=pl.BlockSpec((1, RET_CHUNK, GROUP_WIDTH), lambda bi, ci: (bi, ci, 0)),
        out_shape=jax.ShapeDtypeStruct((b, s, GROUP_WIDTH), F32),
        scratch_shapes=[pltpu.VMEM((N_HEADS, HEAD_DIM, HEAD_DIM), F32)],
        compiler_params=_params(("parallel", "arbitrary")),
        name="retention",
    )(y_ret, intra, qd, kd, cd, nw)


def _rank_desc(vals, n_cand):
    lane = lax.broadcasted_iota(jnp.int32, vals.shape, 1)
    rank = jnp.zeros(vals.shape, F32)
    for m in range(n_cand):
        col = vals[:, m:m + 1]
        beats = jnp.logical_or(col > vals, jnp.logical_and(col == vals, lane > m))
        rank = rank + jnp.where(beats, 1.0, 0.0)
    return rank


def _moba_kernel(q_ref, k_ref, v_ref, o_ref, km_ref):
    ci = pl.program_id(1)
    nb = SEQ // MOBA_BLOCK

    @pl.when(ci == 0)
    def _():
        km_ref[...] = jnp.zeros_like(km_ref)
        for n in range(nb):
            km_ref[n:n + 1, :] = jnp.mean(k_ref[0, n * MOBA_BLOCK:(n + 1) * MOBA_BLOCK, :], axis=0, keepdims=True)

    cur = ci // (MOBA_BLOCK // Q_BLOCK)
    t0 = ci * Q_BLOCK
    tpos = t0 + lax.broadcasted_iota(jnp.int32, (Q_BLOCK, 1), 0)
    lane = lax.broadcasted_iota(jnp.int32, (Q_BLOCK, 128), 1)
    kcol = lax.broadcasted_iota(jnp.int32, (1, MOBA_BLOCK), 1)
    scale = HEAD_DIM ** -0.5

    for h in range(N_HEADS):
        sl = slice(h * HEAD_DIM, (h + 1) * HEAD_DIM)
        q = q_ref[0, :, sl].astype(BF16)
        gate = _dot_nt(q, km_ref[:, sl].astype(BF16))
        gate = jnp.where(lane < cur, gate, NEG)
        rank = _rank_desc(gate, nb)
        allow = jnp.logical_or(jnp.logical_and(rank < MOBA_TOPK, lane < cur), lane == cur)
        allow = jnp.where(allow, 1.0, 0.0)

        def body(n, carry):
            m, l, acc = carry
            start = pl.multiple_of(n * MOBA_BLOCK, MOBA_BLOCK)
            kb = k_ref[0, pl.ds(start, MOBA_BLOCK), sl].astype(BF16)
            vb = v_ref[0, pl.ds(start, MOBA_BLOCK), sl].astype(BF16)
            s = _dot_nt(q, kb) * scale
            acol = jnp.sum(jnp.where(lane == n, allow, 0.0), axis=1, keepdims=True)
            ok = jnp.logical_and(acol > 0.5, (start + kcol) <= tpos)
            s = jnp.where(ok, s, NEG)
            m_new = jnp.maximum(m, jnp.max(s, axis=1, keepdims=True))
            p = jnp.where(ok, jnp.exp(s - m_new), 0.0)
            alpha = jnp.exp(m - m_new)
            l = alpha * l + jnp.sum(p, axis=1, keepdims=True)
            acc = alpha * acc + _dot(p.astype(BF16), vb)
            return m_new, l, acc

        init = (jnp.full((Q_BLOCK, 1), NEG, F32), jnp.zeros((Q_BLOCK, 1), F32),
                jnp.zeros((Q_BLOCK, HEAD_DIM), F32))
        m, l, acc = lax.fori_loop(0, cur + 1, body, init)
        o_ref[0, :, sl] = acc / l


def _moba(y_moba):
    b, s, _ = y_moba.shape
    nq = s // Q_BLOCK
    return pl.pallas_call(
        _moba_kernel,
        grid=(b, nq),
        in_specs=[pl.BlockSpec((1, Q_BLOCK, GROUP_WIDTH), lambda bi, ci: (bi, ci, 0)),
                  pl.BlockSpec((1, s, GROUP_WIDTH), lambda bi, ci: (bi, 0, 1)),
                  pl.BlockSpec((1, s, GROUP_WIDTH), lambda bi, ci: (bi, 0, 2))],
        out_specs=pl.BlockSpec((1, Q_BLOCK, GROUP_WIDTH), lambda bi, ci: (bi, ci, 0)),
        out_shape=jax.ShapeDtypeStruct((b, s, GROUP_WIDTH), F32),
        scratch_shapes=[pltpu.VMEM((128, GROUP_WIDTH), F32)],
        compiler_params=_params(("parallel", "arbitrary")),
        name="moba",
    )(y_moba, y_moba, y_moba)


def _ssd_kernel(z_ref, x_ref, b_ref, c_ref, dt_ref, cw_ref, cb_ref, dtb_ref, alog_ref,
                dskip_ref, nw_ref, tril_ref, o_ref, buf_ref, state_ref):
    L = SSM_CHUNK
    ci = pl.program_id(1)

    @pl.when(ci == 0)
    def _():
        buf_ref[...] = jnp.zeros_like(buf_ref)
        state_ref[...] = jnp.zeros_like(state_ref)

    convd = []
    for j, ref in enumerate((x_ref, b_ref, c_ref)):
        cs = slice(j * GROUP_WIDTH, (j + 1) * GROUP_WIDTH)
        buf_ref[8:8 + L, cs] = ref[0]
        acc = cb_ref[:, cs] + cw_ref[SSM_CONV - 1:SSM_CONV, cs] * ref[0]
        for i in range(SSM_CONV - 1):
            shift = SSM_CONV - 1 - i
            acc = acc + cw_ref[i:i + 1, cs] * buf_ref[8 - shift:8 - shift + L, cs]
        convd.append(_silu(acc))
    buf_ref[0:8, :] = buf_ref[L:L + 8, :]
    xs, bm, cm = convd

    dtr = dt_ref[0] + dtb_ref[...]
    dt = jnp.maximum(dtr, 0.0) + jnp.log1p(jnp.exp(-jnp.abs(dtr)))
    a = -jnp.exp(alog_ref[...])
    da = dt * a
    d1, d2, d3 = da.astype(BF16), None, None
    r1 = da - d1.astype(F32)
    d2 = r1.astype(BF16)
    d3 = (r1 - d2.astype(F32)).astype(BF16)
    tril = tril_ref[...]
    acol = _dot(tril, d1) + _dot(tril, d2) + _dot(tril, d3)
    arow = acol.T
    alast = acol[L - 1:L, :]
    row = lax.broadcasted_iota(jnp.int32, (L, L), 0)
    col = lax.broadcasted_iota(jnp.int32, (L, L), 1)
    causal = row >= col

    gmats = []
    for g in range(2):
        gs = slice(g * SSM_STATE, (g + 1) * SSM_STATE)
        gmats.append(_dot_nt(cm[:, gs].astype(BF16), bm[:, gs].astype(BF16)))

    for h in range(N_HEADS):
        sl = slice(h * HEAD_DIM, (h + 1) * HEAD_DIM)
        g = h // 2
        gs = slice(g * SSM_STATE, (g + 1) * SSM_STATE)
        a_c = acol[:, h:h + 1]
        a_r = arow[h:h + 1, :]
        a_l = alast[:, h:h + 1]
        decay = jnp.where(causal, jnp.exp(jnp.where(causal, a_c - a_r, 0.0)), 0.0)
        xh = xs[:, sl]
        xdt = xh * dt[:, h:h + 1]
        y_diag = _dot((gmats[g] * decay).astype(BF16), xdt.astype(BF16))
        to_end = jnp.exp(a_l - a_c)
        st_new = _dot_tn((xdt * to_end).astype(BF16), bm[:, gs].astype(BF16))
        prev = state_ref[h]
        y_off = _dot_nt(cm[:, gs].astype(BF16), prev.astype(BF16)) * jnp.exp(a_c)
        state_ref[h] = prev * jnp.exp(a_l) + st_new
        o_ref[0, :, sl] = y_diag + y_off + xh * dskip_ref[:, sl]

    y = o_ref[0] * _silu(z_ref[0])
    for g in range(2):
        gs = slice(g * 128, (g + 1) * 128)
        yg = y[:, gs]
        o_ref[0, :, gs] = yg * lax.rsqrt(jnp.mean(yg * yg, axis=-1, keepdims=True) + NORM_EPS) * nw_ref[:, gs]


def _ssd(y_ssm, y_small, cw, cb, dtb, alog, dskip, nw, tril):
    b, s, _ = y_ssm.shape
    nch = s // SSM_CHUNK
    L = SSM_CHUNK
    full = lambda shape: pl.BlockSpec(shape, lambda bi, ci: (0,) * len(shape))
    colblk = lambda j: pl.BlockSpec((1, L, GROUP_WIDTH), lambda bi, ci, j=j: (bi, ci, j))
    return pl.pallas_call(
        _ssd_kernel,
        grid=(b, nch),
        in_specs=[colblk(0), colblk(1), colblk(2), colblk(3),
                  pl.BlockSpec((1, L, W_SMALL), lambda bi, ci: (bi, ci, 0)),
                  full(cw.shape), full(cb.shape), full(dtb.shape), full(alog.shape),
                  full(dskip.shape), full(nw.shape), full(tril.shape)],
        out_specs=pl.BlockSpec((1, L, GROUP_WIDTH), lambda bi, ci: (bi, ci, 0)),
        out_shape=jax.ShapeDtypeStruct((b, s, GROUP_WIDTH), F32),
        scratch_shapes=[pltpu.VMEM((L + 8, 3 * GROUP_WIDTH), F32),
                        pltpu.VMEM((N_HEADS, HEAD_DIM, SSM_STATE), F32)],
        compiler_params=_params(("parallel", "arbitrary")),
        name="ssd",
    )(y_ssm, y_ssm, y_ssm, y_ssm, y_small, cw, cb, dtb, alog, dskip, nw, tril)


def _nsa_cmp_kernel(k_ref, v_ref, pek_ref, w1k_ref, w2k_ref, pev_ref, w1v_ref, w2v_ref, kc_ref, vc_ref):
    half = NSA_CMP_STRIDE * HEAD_DIM
    for x_ref, pe_ref, w1_ref, w2_ref, o_ref in ((k_ref, pek_ref, w1k_ref, w2k_ref, kc_ref),
                                                 (v_ref, pev_ref, w1v_ref, w2v_ref, vc_ref)):
        x = x_ref[0]
        a = _dot((x + pe_ref[:, :half]).astype(BF16), w1_ref[:half, :])
        bb = _dot((x + pe_ref[:, half:]).astype(BF16), w1_ref[half:, :])
        h1 = a + pltpu.roll(bb, bb.shape[0] - 1, 0)
        o_ref[0] = _dot(_gelu(h1).astype(BF16), w2_ref[...])


def _nsa_compress(k16, v16, pek, w1k, w2k, pev, w1v, w2v):
    b = k16.shape[0]
    full = lambda shape: pl.BlockSpec(shape, lambda bi: (0,) * len(shape))
    blk = pl.BlockSpec((1, 128, NSA_CMP_STRIDE * HEAD_DIM), lambda bi: (bi, 0, 0))
    oblk = pl.BlockSpec((1, 128, HEAD_DIM), lambda bi: (bi, 0, 0))
    return pl.pallas_call(
        _nsa_cmp_kernel,
        grid=(b,),
        in_specs=[blk, blk, full(pek.shape), full(w1k.shape), full(w2k.shape),
                  full(pev.shape), full(w1v.shape), full(w2v.shape)],
        out_specs=[oblk, oblk],
        out_shape=[jax.ShapeDtypeStruct((b, 128, HEAD_DIM), F32)] * 2,
        compiler_params=_params(("parallel",)),
        name="nsa_compress",
    )(k16, v16, pek, w1k, w2k, pev, w1v, w2v)


def _softmax_step(carry, s, ok, vb):
    m, l, acc = carry
    s = jnp.where(ok, s, NEG)
    m_new = jnp.maximum(m, jnp.max(s, axis=1, keepdims=True))
    p = jnp.where(ok, jnp.exp(s - m_new), 0.0)
    alpha = jnp.exp(m - m_new)
    l = alpha * l + jnp.sum(p, axis=1, keepdims=True)
    acc = alpha * acc + _dot(p.astype(BF16), vb)
    return m_new, l, acc


def _nsa_kernel(q_ref, kk_ref, vv_ref, kc_ref, vc_ref, g_ref, ovl_ref, o_ref):
    ci = pl.program_id(1)
    t0 = ci * Q_BLOCK
    scale = HEAD_DIM ** -0.5
    R = N_HEADS * Q_BLOCK
    q4 = jnp.concatenate([q_ref[0, :, h * HEAD_DIM:(h + 1) * HEAD_DIM] for h in range(N_HEADS)],
                         axis=0).astype(BF16)
    tq = t0 + lax.broadcasted_iota(jnp.int32, (Q_BLOCK, 1), 0)
    tq4 = jnp.concatenate([tq] * N_HEADS, axis=0)
    lane = lax.broadcasted_iota(jnp.int32, (Q_BLOCK, 128), 1)
    lane4 = lax.broadcasted_iota(jnp.int32, (R, 128), 1)

    s = _dot_nt(q4, kc_ref[0].astype(BF16)) * scale
    ok = (lane4 * NSA_CMP_STRIDE + (NSA_CMP_BLOCK - 1)) <= tq4
    s = jnp.where(ok, s, NEG)
    e = jnp.where(ok, jnp.exp(s - jnp.max(s, axis=1, keepdims=True)), 0.0)
    l = jnp.sum(e, axis=1, keepdims=True)
    p_cmp = e / jnp.where(l > 0.0, l, 1.0)
    pb = p_cmp.astype(BF16)
    o_cmp = _dot(pb, vc_ref[0].astype(BF16))
    imp4 = _dot(pb, ovl_ref[...])
    imp = imp4[0:Q_BLOCK] + imp4[Q_BLOCK:2 * Q_BLOCK] + imp4[2 * Q_BLOCK:3 * Q_BLOCK] + imp4[3 * Q_BLOCK:]

    cur_b = tq // NSA_SLC_BLOCK
    forced = jnp.logical_or(jnp.logical_or(lane == 0, lane == cur_b), lane == cur_b - 1)
    imp = jnp.where(forced, imp + NSA_FORCE_BONUS, imp)
    imp = jnp.where(lane <= cur_b, imp, NEG)
    nsb = SEQ // NSA_SLC_BLOCK
    sel = jnp.where(_rank_desc(imp, nsb) < NSA_SLC_TOPK, 1.0, 0.0).astype(BF16)

    init = (jnp.full((R, 1), NEG, F32), jnp.zeros((R, 1), F32), jnp.zeros((R, HEAD_DIM), F32))

    TS = 256
    jrow = lax.broadcasted_iota(jnp.int32, (128, TS), 0)
    kcol_s = lax.broadcasted_iota(jnp.int32, (1, TS), 1)
    jcol = lax.broadcasted_iota(jnp.int32, (128, TS), 1) // NSA_SLC_BLOCK

    def slc_body(n, carry):
        start = pl.multiple_of(n * TS, TS)
        kb = kk_ref[0, pl.ds(start, TS), 0:HEAD_DIM].astype(BF16)
        vb = vv_ref[0, pl.ds(start, TS), 0:HEAD_DIM].astype(BF16)
        sc = _dot_nt(q4, kb) * scale
        expand = jnp.where(jrow == n * (TS // NSA_SLC_BLOCK) + jcol, 1.0, 0.0).astype(BF16)
        selk = _dot(sel, expand)
        okq = jnp.logical_and(selk > 0.5, (start + kcol_s) <= tq)
        okf = jnp.where(okq, 1.0, 0.0)
        ok4 = jnp.concatenate([okf] * N_HEADS, axis=0) > 0.5
        return _softmax_step(carry, sc, ok4, vb)

    _, l_s, acc_s = lax.fori_loop(0, ci // (TS // Q_BLOCK) + 1, slc_body, init)
    o_slc = acc_s / l_s

    TW = Q_BLOCK
    kcol_w = lax.broadcasted_iota(jnp.int32, (1, TW), 1)

    def win_body(n, carry):
        start = pl.multiple_of(n * TW, TW)
        kb = kk_ref[0, pl.ds(start, TW), HEAD_DIM:2 * HEAD_DIM].astype(BF16)
        vb = vv_ref[0, pl.ds(start, TW), HEAD_DIM:2 * HEAD_DIM].astype(BF16)
        sc = _dot_nt(q4, kb) * scale
        kpos = start + kcol_w
        ok = jnp.logical_and(kpos <= tq4, kpos > tq4 - NSA_WINDOW)
        return _softmax_step(carry, sc, ok, vb)

    _, l_w, acc_w = lax.fori_loop(jnp.maximum(ci - NSA_WINDOW // TW, 0), ci + 1, win_body, init)
    o_win = acc_w / l_w

    gates = 1.0 / (1.0 + jnp.exp(-g_ref[0]))
    for h in range(N_HEADS):
        rs = slice(h * Q_BLOCK, (h + 1) * Q_BLOCK)
        o = (gates[:, 4 + h:5 + h] * o_cmp[rs] + gates[:, 8 + h:9 + h] * o_slc[rs]
             + gates[:, 12 + h:13 + h] * o_win[rs])
        o_ref[0, :, h * HEAD_DIM:(h + 1) * HEAD_DIM] = o


def _nsa_attend(y_nsa, y_small, kc, vc, ovl):
    b, s, _ = y_nsa.shape
    nq = s // Q_BLOCK
    return pl.pallas_call(
        _nsa_kernel,
        grid=(b, nq),
        in_specs=[pl.BlockSpec((1, Q_BLOCK, GROUP_WIDTH), lambda bi, ci: (bi, ci, 0)),
                  pl.BlockSpec((1, s, 128), lambda bi, ci: (bi, 0, 3)),
                  pl.BlockSpec((1, s, 128), lambda bi, ci: (bi, 0, 4)),
                  pl.BlockSpec((1, 128, HEAD_DIM), lambda bi, ci: (bi, 0, 0)),
                  pl.BlockSpec((1, 128, HEAD_DIM), lambda bi, ci: (bi, 0, 0)),
                  pl.BlockSpec((1, Q_BLOCK, W_SMALL), lambda bi, ci: (bi, ci, 0)),
                  pl.BlockSpec((128, 128), lambda bi, ci: (0, 0))],
        out_specs=pl.BlockSpec((1, Q_BLOCK, GROUP_WIDTH), lambda bi, ci: (bi, ci, 0)),
        out_shape=jax.ShapeDtypeStruct((b, s, GROUP_WIDTH), F32),
        compiler_params=_params(("parallel", "parallel")),
        name="nsa_attend",
    )(y_nsa, y_nsa, y_nsa, kc, vc, y_small, ovl)


def _outproj_kernel(x_ref, a_ref, b_ref, c_ref, d_ref, wo_ref, nw_ref, wq_ref, xo_ref, hn_ref, q_ref):
    acc = x_ref[...]
    for j, ref in enumerate((a_ref, b_ref, c_ref, d_ref)):
        acc = acc + _dot(ref[...].astype(BF16), wo_ref[j * GROUP_WIDTH:(j + 1) * GROUP_WIDTH, :])
    xo_ref[...] = acc
    hn = acc * lax.rsqrt(jnp.mean(acc * acc, axis=-1, keepdims=True) + NORM_EPS) * nw_ref[...]
    hn_ref[...] = hn
    q_ref[...] = _dot(hn.astype(BF16), wq_ref[...])


def _outproj(x2, mixed, wo, nw, wq):
    t, d = x2.shape
    nq = wq.shape[1]
    row = lambda wd: pl.BlockSpec((IN_TM, wd), lambda i: (i, 0))
    full = lambda shape: pl.BlockSpec(shape, lambda i: (0,) * len(shape))
    return pl.pallas_call(
        _outproj_kernel,
        grid=(t // IN_TM,),
        in_specs=[row(d)] + [row(GROUP_WIDTH)] * 4 + [full(wo.shape), full(nw.shape), full(wq.shape)],
        out_specs=[row(d), row(d), row(nq)],
        out_shape=[jax.ShapeDtypeStruct((t, d), F32), jax.ShapeDtypeStruct((t, d), F32),
                   jax.ShapeDtypeStruct((t, nq), F32)],
        compiler_params=_params(("parallel",)),
        name="outproj",
    )(x2, *mixed, wo, nw, wq)


def _topk_rows(v, k, payload=None):
    rows = lax.broadcasted_iota(jnp.int32, v.shape, 0).astype(F32)
    big = float(v.shape[0])
    vals, idxs = [], []
    for _ in range(k):
        m = jnp.max(v, axis=0, keepdims=True)
        first = jnp.min(jnp.where(v == m, rows, big), axis=0, keepdims=True)
        hit = rows == first
        vals.append(m)
        if payload is None:
            idxs.append(first)
        else:
            idxs.append(jnp.max(jnp.where(hit, payload, -1.0), axis=0, keepdims=True))
        v = jnp.where(hit, -jnp.inf, v)
    return jnp.concatenate(vals, axis=0), jnp.concatenate(idxs, axis=0)


def _peer_topk_kernel(q_ref, keys_ref, idx_ref, gw_ref):
    q = q_ref[...].astype(BF16)
    s1 = _dot_nt(keys_ref[0, 0].astype(BF16), q[:, :PEER_NKEYS])
    s2 = _dot_nt(keys_ref[0, 1].astype(BF16), q[:, PEER_NKEYS:])
    v1, i1 = _topk_rows(s1, PEER_TOPK)
    v2, i2 = _topk_rows(s2, PEER_TOPK)
    cand = jnp.concatenate([v1[a:a + 1, :] + v2 for a in range(PEER_TOPK)], axis=0)
    cidx = jnp.concatenate([i1[a:a + 1, :] * float(PEER_NKEYS) + i2 for a in range(PEER_TOPK)], axis=0)
    top_s, top_i = _topk_rows(cand, PEER_TOPK, payload=cidx)
    e = jnp.exp(top_s - top_s[0:1, :])
    gw_ref[0] = e / jnp.sum(e, axis=0, keepdims=True)
    idx_ref[0] = top_i.astype(jnp.int32)


def _peer_topk(q, keys):
    t = q.shape[0]
    nblk = t // 128
    shape = jax.ShapeDtypeStruct((nblk, PEER_PICKS, 128), jnp.int32)
    oblk = pl.BlockSpec((1, PEER_TOPK, 128), lambda i, h: (i, h, 0))
    return pl.pallas_call(
        _peer_topk_kernel,
        grid=(nblk, PEER_HEADS),
        in_specs=[pl.BlockSpec((128, 2 * PEER_NKEYS), lambda i, h: (i, h)),
                  pl.BlockSpec((1, 2, PEER_NKEYS, PEER_NKEYS), lambda i, h: (h, 0, 0, 0))],
        out_specs=[oblk, oblk],
        out_shape=[shape, jax.ShapeDtypeStruct(shape.shape, F32)],
        compiler_params=_params(("parallel", "parallel")),
        name="peer_topk",
    )(q, keys)


def _load_table(tab_hbm, tab_ref, sem):
    @pl.when(pl.program_id(0) == 0)
    def _():
        cp = pltpu.make_async_copy(tab_hbm, tab_ref, sem)
        cp.start()
        cp.wait()


def _unpack(words):
    lo = lax.bitcast_convert_type(lax.shift_left(words, jnp.uint32(16)), F32)
    hi = lax.bitcast_convert_type(jnp.bitwise_and(words, jnp.uint32(0xFFFF0000)), F32)
    return lo, hi


def _peer_act_kernel(idx_ref, x_ref, gw_ref, tab_hbm, o_ref, tab_ref, prod_ref, sem):
    _load_table(tab_hbm, tab_ref, sem)
    grp = (lax.broadcasted_iota(jnp.int32, (PEER_PICKS, 4 * PEER_PICKS), 1) // 4
           == lax.broadcasted_iota(jnp.int32, (PEER_PICKS, 4 * PEER_PICKS), 0))
    grp = jnp.where(grp, 1.0, 0.0).astype(BF16)
    ones = jnp.ones((8, 128), BF16)

    def token(t, _):
        xlo = x_ref[t, 0:4, :]
        xhi = x_ref[t, 4:8, :]
        for k in range(PEER_PICKS):
            r = pl.multiple_of(idx_ref[t, k] * 4, 4)
            lo, hi = _unpack(tab_ref[pl.ds(r, 4), :])
            prod_ref[4 * k:4 * k + 4, :] = lo * xlo + hi * xhi
        p_hi, p_lo = _split_bf16(prod_ref[...])
        gp = _dot(grp, p_hi) + _dot(grp, p_lo)
        g_hi, g_lo = _split_bf16(gp)
        act = (_dot_nt(ones, g_hi) + _dot_nt(ones, g_lo))[0:1, :]
        o_ref[pl.ds(t, 1), :] = _gelu(act) * gw_ref[pl.ds(t, 1), :]
        return 0

    lax.fori_loop(0, PEER_TB, token, 0)


def _peer_act(idx, x8, gw, table):
    t = idx.shape[0]
    return pl.pallas_call(
        _peer_act_kernel,
        grid=(t // PEER_TB,),
        in_specs=[pl.BlockSpec((PEER_TB, PEER_PICKS), lambda i: (i, 0), memory_space=pltpu.SMEM),
                  pl.BlockSpec((PEER_TB, 8, 128), lambda i: (i, 0, 0)),
                  pl.BlockSpec((PEER_TB, PEER_PICKS), lambda i: (i, 0)),
                  pl.BlockSpec(memory_space=pl.ANY)],
        out_specs=pl.BlockSpec((PEER_TB, PEER_PICKS), lambda i: (i, 0)),
        out_shape=jax.ShapeDtypeStruct((t, PEER_PICKS), F32),
        scratch_shapes=[pltpu.VMEM(table.shape, jnp.uint32),
                        pltpu.VMEM((4 * PEER_PICKS, 128), F32),
                        pltpu.SemaphoreType.DMA(())],
        compiler_params=_params(("arbitrary",)),
        name="peer_act",
    )(idx, x8, gw, table)


def _peer_out_kernel(idx_ref, w_ref, x_ref, tab_hbm, o_ref, tab_ref, rows_ref, sem):
    _load_table(tab_hbm, tab_ref, sem)
    n = 4 * PEER_PICKS
    rep = (lax.broadcasted_iota(jnp.int32, (PEER_PICKS, n), 1) // 4
           == lax.broadcasted_iota(jnp.int32, (PEER_PICKS, n), 0))
    rep = jnp.where(rep, 1.0, 0.0).astype(BF16)
    pattern = (lax.broadcasted_iota(jnp.int32, (8, n), 1) % 4) == lax.broadcasted_iota(jnp.int32, (8, n), 0)

    def token(t, _):
        for k in range(PEER_PICKS):
            r = pl.multiple_of(idx_ref[t, k] * 4, 4)
            rows_ref[4 * k:4 * k + 4, :] = tab_ref[pl.ds(r, 4), :]
        lo, hi = _unpack(rows_ref[...])
        lo, hi = lo.astype(BF16), hi.astype(BF16)
        w8 = jnp.broadcast_to(w_ref[pl.ds(t, 1), :], (8, PEER_PICKS))
        w_hi, w_lo = _split_bf16(w8)
        out_lo = jnp.zeros((8, 128), F32)
        out_hi = jnp.zeros((8, 128), F32)
        for wpart in (w_hi, w_lo):
            sel = jnp.where(pattern, _dot(wpart, rep), 0.0).astype(BF16)
            out_lo = out_lo + _dot(sel, lo)
            out_hi = out_hi + _dot(sel, hi)
        o_ref[t] = x_ref[t] + jnp.concatenate([out_lo[0:4], out_hi[0:4]], axis=0)
        return 0

    lax.fori_loop(0, PEER_TB, token, 0)


def _peer_out(idx, w, x8, table):
    t = idx.shape[0]
    return pl.pallas_call(
        _peer_out_kernel,
        grid=(t // PEER_TB,),
        in_specs=[pl.BlockSpec((PEER_TB, PEER_PICKS), lambda i: (i, 0), memory_space=pltpu.SMEM),
                  pl.BlockSpec((PEER_TB, PEER_PICKS), lambda i: (i, 0)),
                  pl.BlockSpec((PEER_TB, 8, 128), lambda i: (i, 0, 0)),
                  pl.BlockSpec(memory_space=pl.ANY)],
        out_specs=pl.BlockSpec((PEER_TB, 8, 128), lambda i: (i, 0, 0)),
        out_shape=jax.ShapeDtypeStruct((t, 8, 128), F32),
        scratch_shapes=[pltpu.VMEM(table.shape, jnp.uint32),
                        pltpu.VMEM((4 * PEER_PICKS, 128), jnp.uint32),
                        pltpu.SemaphoreType.DMA(())],
        compiler_params=_params(("arbitrary",)),
        name="peer_out",
    )(idx, w, x8, table)


def _pack_table(tab):
    e, d = tab.shape
    bits = lax.bitcast_convert_type(tab.astype(jnp.bfloat16), jnp.uint16).astype(jnp.uint32)
    words = bits[:, :d // 2] | (bits[:, d // 2:] << 16)
    return words.reshape(e * 4, 128)


def _norm_kernel(x_ref, w_ref, o_ref):
    x = x_ref[...]
    o_ref[...] = x * lax.rsqrt(jnp.mean(x * x, axis=-1, keepdims=True) + NORM_EPS) * w_ref[...]


def _final_norm(x2, w):
    t, d = x2.shape
    tm = 512
    return pl.pallas_call(
        _norm_kernel,
        grid=(t // tm,),
        in_specs=[pl.BlockSpec((tm, d), lambda i: (i, 0)), pl.BlockSpec((1, d), lambda i: (0, 0))],
        out_specs=pl.BlockSpec((tm, d), lambda i: (i, 0)),
        out_shape=jax.ShapeDtypeStruct((t, d), F32),
        compiler_params=_params(("parallel",)),
        name="final_norm",
    )(x2, w)


def _permute_w_in(w):
    cols = [w[:, :2816],
            w[:, 2820:3076],
            w[:, 3076:3140], w[:, 3140:3204],
            w[:, 3204:3268], w[:, 3332:3396],
            w[:, 3268:3332], w[:, 3396:3460],
            w[:, 2816:2820], w[:, 3460:3472],
            jnp.zeros((w.shape[0], W_SMALL - 16), w.dtype)]
    return jnp.concatenate(cols, axis=1).astype(BF16)


def _pad_lanes(v, n=128):
    return jnp.pad(v, (0, n - v.shape[0]))[None, :]


def kernel(x, norm_mix, w_in, ret_norm, ssm_conv_w, ssm_conv_b, ssm_dt_bias, ssm_a_log, ssm_d, ssm_norm, nsa_pe_k, nsa_w1_k, nsa_w2_k, nsa_pe_v, nsa_w1_v, nsa_w2_v, w_out, norm_ffn, peer_wq, peer_keys, peer_u, peer_v, norm_final):
    b, s, d = x.shape
    t = b * s
    depth = w_in.shape[0]
    rc, ra, rb = (jnp.asarray(a) for a in _rope_tables())
    intra, qd, kd, cd = (jnp.asarray(a) for a in _ret_tables())
    ovl = jnp.asarray(_nsa_overlap()).astype(BF16)
    tril = jnp.asarray(np.tril(np.ones((SSM_CHUNK, SSM_CHUNK), np.float32))).astype(BF16)

    for i in range(depth):
        y_ret, y_moba, y_ssm, y_nsa, y_small = _inproj(
            x, norm_mix[i][None, :], _permute_w_in(w_in[i]), rc, ra, rb)
        o_ret = _retention(y_ret, intra, qd, kd, cd, ret_norm[i][None, :])
        o_moba = _moba(y_moba)
        o_ssm = _ssd(y_ssm, y_small, ssm_conv_w[i], ssm_conv_b[i][None, :],
                     _pad_lanes(jnp.pad(ssm_dt_bias[i], (0, 0))), _pad_lanes(ssm_a_log[i]),
                     jnp.repeat(ssm_d[i], HEAD_DIM)[None, :], ssm_norm[i][None, :], tril)
        k16 = y_nsa[:, :, 256:320].reshape(b, s // NSA_CMP_STRIDE, NSA_CMP_STRIDE * HEAD_DIM)
        v16 = y_nsa[:, :, 320:384].reshape(b, s // NSA_CMP_STRIDE, NSA_CMP_STRIDE * HEAD_DIM)
        kc, vc = _nsa_compress(k16, v16,
                               nsa_pe_k[i].reshape(1, -1), nsa_w1_k[i].astype(BF16), nsa_w2_k[i].astype(BF16),
                               nsa_pe_v[i].reshape(1, -1), nsa_w1_v[i].astype(BF16), nsa_w2_v[i].astype(BF16))
        o_nsa = _nsa_attend(y_nsa, y_small, kc, vc, ovl)
        mixed = [o.reshape(t, GROUP_WIDTH) for o in (o_ret, o_moba, o_ssm, o_nsa)]
        x_mid, hn, q = _outproj(x.reshape(t, d), mixed, w_out[i].astype(BF16), norm_ffn[i][None, :],
                                peer_wq[i].astype(BF16))
        idx_t, gw_t = _peer_topk(q, peer_keys[i])
        idx = idx_t.transpose(0, 2, 1).reshape(t, PEER_PICKS)
        gw = gw_t.transpose(0, 2, 1).reshape(t, PEER_PICKS)
        wts = _peer_act(idx, hn.reshape(t, 8, 128), gw, _pack_table(peer_u[i]))
        x = _peer_out(idx, wts, x_mid.reshape(t, 8, 128), _pack_table(peer_v[i])).reshape(b, s, d)
    return _final_norm(x.reshape(t, d), norm_final[None, :]).reshape(b, s, d)
```

```python
import functools
import math

import numpy as np
import jax
import jax.numpy as jnp
from jax import lax
from jax.experimental import pallas as pl
from jax.experimental.pallas import tpu as pltpu

F32 = jnp.float32
BF16 = jnp.bfloat16

D_MODEL = 1024
SEQ = 2048
GROUP_WIDTH = 256
HEAD_DIM = 64
N_HEADS = 4
NORM_EPS = 1e-6
NEG = -1e30

ROPE_THETA = 500000.0
ROPE_DIMS = 16
RET_THETA = 10000.0
RET_CHUNK = 128
MOBA_BLOCK = 256
MOBA_TOPK = 3
SSM_CHUNK = 256
SSM_STATE = 128
SSM_CONV = 4
NSA_CMP_BLOCK = 32
NSA_CMP_STRIDE = 16
NSA_SLC_BLOCK = 64
NSA_SLC_TOPK = 16
NSA_WINDOW = 512
NSA_FORCE_BONUS = 1e6
PEER_HEADS = 8
PEER_NKEYS = 128
PEER_TOPK = 16
PEER_PICKS = PEER_HEADS * PEER_TOPK
PEER_TOPK_HEADS = 4
Q_BLOCK = 128

W_RET, W_MOBA, W_SSM, W_NSA, W_SMALL = 1024, 768, 1024, 640, 128
IN_COLS_PADDED = W_RET + W_MOBA + W_SSM + W_NSA + W_SMALL
ROPE_W = 512

IN_TM = 256
PEER_TB = 32
VMEM_LIMIT = 56 * 1024 * 1024


def _params(sem, vmem=VMEM_LIMIT):
    return pltpu.CompilerParams(dimension_semantics=sem, vmem_limit_bytes=vmem)


def _dot(a, b):
    return jnp.dot(a, b, preferred_element_type=F32)


def _dot_nt(a, b):
    return lax.dot_general(a, b, (((1,), (1,)), ((), ())), preferred_element_type=F32)


def _dot_tn(a, b):
    return lax.dot_general(a, b, (((0,), (0,)), ((), ())), preferred_element_type=F32)


def _split_bf16(x):
    hi = x.astype(BF16)
    lo = (x - hi.astype(F32)).astype(BF16)
    return hi, lo


def _silu(x):
    return x * (1.0 / (1.0 + jnp.exp(-x)))


def _gelu(x):
    return 0.5 * x * (1.0 + lax.erf(x * (1.0 / math.sqrt(2.0))))


def _rope_region(seq, theta, rot_dims, n_blocks, scale_blocks=(), ident_blocks=()):
    half = rot_dims // 2
    inv = 1.0 / (theta ** (np.arange(0, rot_dims, 2, dtype=np.float64) / rot_dims))
    ang = np.arange(seq, dtype=np.float64)[:, None] * inv[None, :]
    cos, sin = np.cos(ang), np.sin(ang)
    c = np.ones((seq, HEAD_DIM)); a = np.zeros((seq, HEAD_DIM)); b = np.zeros((seq, HEAD_DIM))
    c[:, :half] = cos; c[:, half:rot_dims] = cos
    a[:, :half] = -sin
    b[:, half:rot_dims] = sin
    cs, as_, bs = [], [], []
    for blk in range(n_blocks):
        if blk in ident_blocks:
            cs.append(np.ones_like(c)); as_.append(np.zeros_like(a)); bs.append(np.zeros_like(b))
            continue
        f = scale_blocks.get(blk, 1.0) if isinstance(scale_blocks, dict) else 1.0
        cs.append(c * f); as_.append(a * f); bs.append(b * f)
    return np.concatenate(cs, 1), np.concatenate(as_, 1), np.concatenate(bs, 1)


@functools.lru_cache(maxsize=None)
def _rope_tables():
    kscale = HEAD_DIM ** -0.5
    ret = _rope_region(SEQ, RET_THETA, HEAD_DIM, 8, scale_blocks={4: kscale, 5: kscale, 6: kscale, 7: kscale})
    moba = _rope_region(SEQ, ROPE_THETA, ROPE_DIMS, 8)
    nsa = _rope_region(SEQ, ROPE_THETA, ROPE_DIMS, 8, ident_blocks=(5,))
    return tuple(np.concatenate([ret[i], moba[i], nsa[i]], 1).astype(np.float32) for i in range(3))


@functools.lru_cache(maxsize=None)
def _ret_tables():
    h, c = N_HEADS, RET_CHUNK
    log_gamma = np.log1p(-np.exp2(-5.0 - np.arange(h, dtype=np.float64)))
    pos = np.arange(c, dtype=np.float64)
    diff = pos[:, None] - pos[None, :]
    intra = np.where(diff >= 0, np.exp(log_gamma[:, None, None] * np.maximum(diff, 0.0)), 0.0)
    q_dec = np.exp(log_gamma[:, None] * (pos + 1.0))
    k_dec = np.exp(log_gamma[:, None] * (c - 1.0 - pos))
    c_dec = np.exp(log_gamma * c)
    qd = np.repeat(q_dec.T, HEAD_DIM, axis=1)
    kd = np.repeat(k_dec.T, HEAD_DIM, axis=1)
    cd = np.broadcast_to(c_dec[:, None, None], (h, HEAD_DIM, HEAD_DIM))
    return (intra.astype(np.float32), qd.astype(np.float32), kd.astype(np.float32),
            np.ascontiguousarray(cd).astype(np.float32))


@functools.lru_cache(maxsize=None)
def _nsa_overlap():
    nc = (SEQ - NSA_CMP_BLOCK) // NSA_CMP_STRIDE + 1
    starts = np.arange(128) * NSA_CMP_STRIDE
    sb = np.arange(128)
    ov = ((starts[:, None] < (sb[None, :] + 1) * NSA_SLC_BLOCK)
          & ((starts + NSA_CMP_BLOCK)[:, None] > sb[None, :] * NSA_SLC_BLOCK))
    ov = ov & (np.arange(128)[:, None] < nc) & (sb[None, :] < SEQ // NSA_SLC_BLOCK)
    return ov.astype(np.float32)


def _inproj_kernel(x_ref, nw_ref, w_ref, rc_ref, ra_ref, rb_ref,
                   o_ret, o_moba, o_ssm, o_nsa, o_small):
    x = x_ref[0]
    ms = jnp.mean(x * x, axis=-1, keepdims=True)
    h = (x * lax.rsqrt(ms + NORM_EPS) * nw_ref[...]).astype(BF16)

    def rope(y, region, shift):
        sl = slice(region * ROPE_W, (region + 1) * ROPE_W)
        return (y * rc_ref[:, sl] + pltpu.roll(y, ROPE_W - shift, 1) * ra_ref[:, sl]
                + pltpu.roll(y, shift, 1) * rb_ref[:, sl])

    off = 0
    y = _dot(h, w_ref[:, off:off + W_RET]); off += W_RET
    o_ret[0, :, :ROPE_W] = rope(y[:, :ROPE_W], 0, HEAD_DIM // 2)
    o_ret[0, :, ROPE_W:] = y[:, ROPE_W:]
    y = _dot(h, w_ref[:, off:off + W_MOBA]); off += W_MOBA
    o_moba[0, :, :ROPE_W] = rope(y[:, :ROPE_W], 1, ROPE_DIMS // 2)
    o_moba[0, :, ROPE_W:] = y[:, ROPE_W:]
    o_ssm[0] = _dot(h, w_ref[:, off:off + W_SSM]); off += W_SSM
    y = _dot(h, w_ref[:, off:off + W_NSA]); off += W_NSA
    o_nsa[0, :, :ROPE_W] = rope(y[:, :ROPE_W], 2, ROPE_DIMS // 2)
    o_nsa[0, :, ROPE_W:] = y[:, ROPE_W:]
    o_small[0] = _dot(h, w_ref[:, off:off + W_SMALL])


def _inproj(x, nw, w, rc, ra, rb):
    b, s, d = x.shape
    ns = s // IN_TM
    widths = (W_RET, W_MOBA, W_SSM, W_NSA, W_SMALL)
    tab = pl.BlockSpec((IN_TM, 3 * ROPE_W), lambda si, bi: (si, 0))
    return pl.pallas_call(
        _inproj_kernel,
        grid=(ns, b),
        in_specs=[pl.BlockSpec((1, IN_TM, d), lambda si, bi: (bi, si, 0)),
                  pl.BlockSpec((1, d), lambda si, bi: (0, 0)),
                  pl.BlockSpec((d, IN_COLS_PADDED), lambda si, bi: (0, 0)),
                  tab, tab, tab],
        out_specs=[pl.BlockSpec((1, IN_TM, wd), lambda si, bi: (bi, si, 0)) for wd in widths],
        out_shape=[jax.ShapeDtypeStruct((b, s, wd), F32) for wd in widths],
        compiler_params=_params(("parallel", "parallel")),
        name="inproj",
    )(x, nw, w, rc, ra, rb)


def _ret_kernel(y_ref, intra_ref, qd_ref, kd_ref, cd_ref, nw_ref, o_ref, state_ref):
    @pl.when(pl.program_id(1) == 0)
    def _():
        state_ref[...] = jnp.zeros_like(state_ref)

    for h in range(N_HEADS):
        sl = slice(h * HEAD_DIM, (h + 1) * HEAD_DIM)
        q = y_ref[0, :, h * HEAD_DIM:(h + 1) * HEAD_DIM].astype(BF16)
        k = y_ref[0, :, GROUP_WIDTH + h * HEAD_DIM:GROUP_WIDTH + (h + 1) * HEAD_DIM]
        v = y_ref[0, :, 2 * GROUP_WIDTH + h * HEAD_DIM:2 * GROUP_WIDTH + (h + 1) * HEAD_DIM].astype(BF16)
        g = y_ref[0, :, 3 * GROUP_WIDTH + h * HEAD_DIM:3 * GROUP_WIDTH + (h + 1) * HEAD_DIM]
        att = _dot_nt(q, k.astype(BF16)) * intra_ref[h]
        o_in = _dot(att.astype(BF16), v)
        st = state_ref[h]
        o_x = _dot(q, st.astype(BF16)) * qd_ref[:, sl]
        kv = _dot_tn((k * kd_ref[:, sl]).astype(BF16), v)
        state_ref[h] = st * cd_ref[h] + kv
        o = o_in + o_x
        o = o * lax.rsqrt(jnp.mean(o * o, axis=-1, keepdims=True) + NORM_EPS) * nw_ref[:, sl]
        o_ref[0, :, sl] = _silu(g) * o


def _retention(y_ret, intra, qd, kd, cd, nw):
    b, s, _ = y_ret.shape
    nch = s // RET_CHUNK
    full = lambda shape: pl.BlockSpec(shape, lambda bi, ci: (0,) * len(shape))
    return pl.pallas_call(
        _ret_kernel,
        grid=(b, nch),
        in_specs=[pl.BlockSpec((1, RET_CHUNK, W_RET), lambda bi, ci: (bi, ci, 0)),
                  full(intra.shape), full(qd.shape), full(kd.shape), full(cd.shape), full(nw.shape)],
        out_specs=pl.BlockSpec((1, RET_CHUNK, GROUP_WIDTH), lambda bi, ci: (bi, ci, 0)),
        out_shape=jax.ShapeDtypeStruct((b, s, GROUP_WIDTH), F32),
        scratch_shapes=[pltpu.VMEM((N_HEADS, HEAD_DIM, HEAD_DIM), F32)],
        compiler_params=_params(("parallel", "arbitrary")),
        name="retention",
    )(y_ret, intra, qd, kd, cd, nw)


def _softmax_step(carry, s, ok, vb):
    m, l, acc = carry
    s = jnp.where(ok, s, NEG)
    m_new = jnp.maximum(m, jnp.max(s, axis=1, keepdims=True))
    p = jnp.where(ok, jnp.exp(s - m_new), 0.0)
    alpha = jnp.exp(m - m_new)
    l = alpha * l + jnp.sum(p, axis=1, keepdims=True)
    acc = alpha * acc + _dot(p.astype(BF16), vb)
    return m_new, l, acc


def _rank_desc(vals, n_cand):
    lane = lax.broadcasted_iota(jnp.int32, vals.shape, 1)
    rank = jnp.zeros(vals.shape, F32)
    for m in range(n_cand):
        col = vals[:, m:m + 1]
        beats = jnp.logical_or(col > vals, jnp.logical_and(col == vals, lane > m))
        rank = rank + jnp.where(beats, 1.0, 0.0)
    return rank


def _moba_kernel(q_ref, k_ref, v_ref, o_ref, km_ref):
    ci = pl.program_id(1)
    nb = SEQ // MOBA_BLOCK

    @pl.when(ci == 0)
    def _():
        km_ref[...] = jnp.zeros_like(km_ref)
        for n in range(nb):
            km_ref[n:n + 1, :] = jnp.mean(k_ref[0, n * MOBA_BLOCK:(n + 1) * MOBA_BLOCK, :], axis=0, keepdims=True)

    cur = ci // (MOBA_BLOCK // Q_BLOCK)
    t0 = ci * Q_BLOCK
    tpos = t0 + lax.broadcasted_iota(jnp.int32, (Q_BLOCK, 1), 0)
    lane = lax.broadcasted_iota(jnp.int32, (Q_BLOCK, 128), 1)
    kcol = lax.broadcasted_iota(jnp.int32, (1, MOBA_BLOCK), 1)
    scale = HEAD_DIM ** -0.5

    heads = [slice(h * HEAD_DIM, (h + 1) * HEAD_DIM) for h in range(N_HEADS)]
    qs, allows = [], []
    for sl in heads:
        q = q_ref[0, :, sl].astype(BF16)
        gate = _dot_nt(q, km_ref[:, sl].astype(BF16))
        gate = jnp.where(lane < cur, gate, NEG)
        rank = _rank_desc(gate, nb)
        allow = jnp.logical_or(jnp.logical_and(rank < MOBA_TOPK, lane < cur), lane == cur)
        qs.append(q)
        allows.append(jnp.where(allow, 1.0, 0.0))

    def body(n, carry):
        start = pl.multiple_of(n * MOBA_BLOCK, MOBA_BLOCK)
        causal = (start + kcol) <= tpos
        out = []
        for sl, q, allow, (m, l, acc) in zip(heads, qs, allows, carry):
            kb = k_ref[0, pl.ds(start, MOBA_BLOCK), sl].astype(BF16)
            vb = v_ref[0, pl.ds(start, MOBA_BLOCK), sl].astype(BF16)
            acol = jnp.sum(jnp.where(lane == n, allow, 0.0), axis=1, keepdims=True)
            ok = jnp.logical_and(acol > 0.5, causal)
            out.append(_softmax_step((m, l, acc), _dot_nt(q, kb) * scale, ok, vb))
        return tuple(out)

    init = (jnp.full((Q_BLOCK, 1), NEG, F32), jnp.zeros((Q_BLOCK, 1), F32),
            jnp.zeros((Q_BLOCK, HEAD_DIM), F32))
    final = lax.fori_loop(0, cur + 1, body, (init,) * N_HEADS)
    for sl, (m, l, acc) in zip(heads, final):
        o_ref[0, :, sl] = acc / l


def _moba(y_moba):
    b, s, _ = y_moba.shape
    nq = s // Q_BLOCK
    return pl.pallas_call(
        _moba_kernel,
        grid=(b, nq),
        in_specs=[pl.BlockSpec((1, Q_BLOCK, GROUP_WIDTH), lambda bi, ci: (bi, ci, 0)),
                  pl.BlockSpec((1, s, GROUP_WIDTH), lambda bi, ci: (bi, 0, 1)),
                  pl.BlockSpec((1, s, GROUP_WIDTH), lambda bi, ci: (bi, 0, 2))],
        out_specs=pl.BlockSpec((1, Q_BLOCK, GROUP_WIDTH), lambda bi, ci: (bi, ci, 0)),
        out_shape=jax.ShapeDtypeStruct((b, s, GROUP_WIDTH), F32),
        scratch_shapes=[pltpu.VMEM((128, GROUP_WIDTH), F32)],
        compiler_params=_params(("parallel", "arbitrary")),
        name="moba",
    )(y_moba, y_moba, y_moba)


def _ssd_kernel(z_ref, x_ref, b_ref, c_ref, dt_ref, cw_ref, cb_ref, dtb_ref, alog_ref,
                dskip_ref, nw_ref, tril_ref, o_ref, buf_ref, state_ref):
    L = SSM_CHUNK
    ci = pl.program_id(1)

    @pl.when(ci == 0)
    def _():
        buf_ref[...] = jnp.zeros_like(buf_ref)
        state_ref[...] = jnp.zeros_like(state_ref)

    convd = []
    for j, ref in enumerate((x_ref, b_ref, c_ref)):
        cs = slice(j * GROUP_WIDTH, (j + 1) * GROUP_WIDTH)
        buf_ref[8:8 + L, cs] = ref[0]
        acc = cb_ref[:, cs] + cw_ref[SSM_CONV - 1:SSM_CONV, cs] * ref[0]
        for i in range(SSM_CONV - 1):
            shift = SSM_CONV - 1 - i
            acc = acc + cw_ref[i:i + 1, cs] * buf_ref[8 - shift:8 - shift + L, cs]
        convd.append(_silu(acc))
    buf_ref[0:8, :] = buf_ref[L:L + 8, :]
    xs, bm, cm = convd

    dtr = dt_ref[0] + dtb_ref[...]
    dt = jnp.maximum(dtr, 0.0) + jnp.log1p(jnp.exp(-jnp.abs(dtr)))
    a = -jnp.exp(alog_ref[...])
    da = dt * a
    d1, d2, d3 = da.astype(BF16), None, None
    r1 = da - d1.astype(F32)
    d2 = r1.astype(BF16)
    d3 = (r1 - d2.astype(F32)).astype(BF16)
    tril = tril_ref[...]
    acol = _dot(tril, d1) + _dot(tril, d2) + _dot(tril, d3)
    arow = acol.T
    alast = acol[L - 1:L, :]
    row = lax.broadcasted_iota(jnp.int32, (L, L), 0)
    col = lax.broadcasted_iota(jnp.int32, (L, L), 1)
    causal = row >= col

    gmats = []
    for g in range(2):
        gs = slice(g * SSM_STATE, (g + 1) * SSM_STATE)
        gmats.append(_dot_nt(cm[:, gs].astype(BF16), bm[:, gs].astype(BF16)))

    for h in range(N_HEADS):
        sl = slice(h * HEAD_DIM, (h + 1) * HEAD_DIM)
        g = h // 2
        gs = slice(g * SSM_STATE, (g + 1) * SSM_STATE)
        a_c = acol[:, h:h + 1]
        a_r = arow[h:h + 1, :]
        a_l = alast[:, h:h + 1]
        decay = jnp.where(causal, jnp.exp(jnp.where(causal, a_c - a_r, 0.0)), 0.0)
        xh = xs[:, sl]
        xdt = xh * dt[:, h:h + 1]
        y_diag = _dot((gmats[g] * decay).astype(BF16), xdt.astype(BF16))
        to_end = jnp.exp(a_l - a_c)
        st_new = _dot_tn((xdt * to_end).astype(BF16), bm[:, gs].astype(BF16))
        prev = state_ref[h]
        y_off = _dot_nt(cm[:, gs].astype(BF16), prev.astype(BF16)) * jnp.exp(a_c)
        state_ref[h] = prev * jnp.exp(a_l) + st_new
        o_ref[0, :, sl] = y_diag + y_off + xh * dskip_ref[:, sl]

    y = o_ref[0] * _silu(z_ref[0])
    for g in range(2):
        gs = slice(g * 128, (g + 1) * 128)
        yg = y[:, gs]
        o_ref[0, :, gs] = yg * lax.rsqrt(jnp.mean(yg * yg, axis=-1, keepdims=True) + NORM_EPS) * nw_ref[:, gs]


def _ssd(y_ssm, y_small, cw, cb, dtb, alog, dskip, nw, tril):
    b, s, _ = y_ssm.shape
    nch = s // SSM_CHUNK
    L = SSM_CHUNK
    full = lambda shape: pl.BlockSpec(shape, lambda bi, ci: (0,) * len(shape))
    colblk = lambda j: pl.BlockSpec((1, L, GROUP_WIDTH), lambda bi, ci, j=j: (bi, ci, j))
    return pl.pallas_call(
        _ssd_kernel,
        grid=(b, nch),
        in_specs=[colblk(0), colblk(1), colblk(2), colblk(3),
                  pl.BlockSpec((1, L, W_SMALL), lambda bi, ci: (bi, ci, 0)),
                  full(cw.shape), full(cb.shape), full(dtb.shape), full(alog.shape),
                  full(dskip.shape), full(nw.shape), full(tril.shape)],
        out_specs=pl.BlockSpec((1, L, GROUP_WIDTH), lambda bi, ci: (bi, ci, 0)),
        out_shape=jax.ShapeDtypeStruct((b, s, GROUP_WIDTH), F32),
        scratch_shapes=[pltpu.VMEM((L + 8, 3 * GROUP_WIDTH), F32),
                        pltpu.VMEM((N_HEADS, HEAD_DIM, SSM_STATE), F32)],
        compiler_params=_params(("parallel", "arbitrary")),
        name="ssd",
    )(y_ssm, y_ssm, y_ssm, y_ssm, y_small, cw, cb, dtb, alog, dskip, nw, tril)


def _nsa_cmp_kernel(k_ref, v_ref, pek_ref, w1k_ref, w2k_ref, pev_ref, w1v_ref, w2v_ref, kc_ref, vc_ref):
    half = NSA_CMP_STRIDE * HEAD_DIM
    for x_ref, pe_ref, w1_ref, w2_ref, o_ref in ((k_ref, pek_ref, w1k_ref, w2k_ref, kc_ref),
                                                 (v_ref, pev_ref, w1v_ref, w2v_ref, vc_ref)):
        x = x_ref[0]
        a = _dot((x + pe_ref[:, :half]).astype(BF16), w1_ref[:half, :])
        bb = _dot((x + pe_ref[:, half:]).astype(BF16), w1_ref[half:, :])
        h1 = a + pltpu.roll(bb, bb.shape[0] - 1, 0)
        o_ref[0] = _dot(_gelu(h1).astype(BF16), w2_ref[...])


def _nsa_compress(k16, v16, pek, w1k, w2k, pev, w1v, w2v):
    b = k16.shape[0]
    full = lambda shape: pl.BlockSpec(shape, lambda bi: (0,) * len(shape))
    blk = pl.BlockSpec((1, 128, NSA_CMP_STRIDE * HEAD_DIM), lambda bi: (bi, 0, 0))
    oblk = pl.BlockSpec((1, 128, HEAD_DIM), lambda bi: (bi, 0, 0))
    return pl.pallas_call(
        _nsa_cmp_kernel,
        grid=(b,),
        in_specs=[blk, blk, full(pek.shape), full(w1k.shape), full(w2k.shape),
                  full(pev.shape), full(w1v.shape), full(w2v.shape)],
        out_specs=[oblk, oblk],
        out_shape=[jax.ShapeDtypeStruct((b, 128, HEAD_DIM), F32)] * 2,
        compiler_params=_params(("parallel",)),
        name="nsa_compress",
    )(k16, v16, pek, w1k, w2k, pev, w1v, w2v)


def _nsa_kernel(q_ref, kk_ref, vv_ref, kc_ref, vc_ref, g_ref, ovl_ref, o_ref):
    ci = pl.program_id(1)
    t0 = ci * Q_BLOCK
    scale = HEAD_DIM ** -0.5
    R = N_HEADS * Q_BLOCK
    q4 = jnp.concatenate([q_ref[0, :, h * HEAD_DIM:(h + 1) * HEAD_DIM] for h in range(N_HEADS)],
                         axis=0).astype(BF16)
    tq = t0 + lax.broadcasted_iota(jnp.int32, (Q_BLOCK, 1), 0)
    tq4 = jnp.concatenate([tq] * N_HEADS, axis=0)
    lane = lax.broadcasted_iota(jnp.int32, (Q_BLOCK, 128), 1)
    lane4 = lax.broadcasted_iota(jnp.int32, (R, 128), 1)

    s = _dot_nt(q4, kc_ref[0].astype(BF16)) * scale
    ok = (lane4 * NSA_CMP_STRIDE + (NSA_CMP_BLOCK - 1)) <= tq4
    s = jnp.where(ok, s, NEG)
    e = jnp.where(ok, jnp.exp(s - jnp.max(s, axis=1, keepdims=True)), 0.0)
    l = jnp.sum(e, axis=1, keepdims=True)
    p_cmp = e / jnp.where(l > 0.0, l, 1.0)
    pb = p_cmp.astype(BF16)
    o_cmp = _dot(pb, vc_ref[0].astype(BF16))
    imp4 = _dot(pb, ovl_ref[...])
    imp = imp4[0:Q_BLOCK] + imp4[Q_BLOCK:2 * Q_BLOCK] + imp4[2 * Q_BLOCK:3 * Q_BLOCK] + imp4[3 * Q_BLOCK:]

    cur_b = tq // NSA_SLC_BLOCK
    forced = jnp.logical_or(jnp.logical_or(lane == 0, lane == cur_b), lane == cur_b - 1)
    imp = jnp.where(forced, imp + NSA_FORCE_BONUS, imp)
    imp = jnp.where(lane <= cur_b, imp, NEG)
    nsb = SEQ // NSA_SLC_BLOCK
    sel = jnp.where(_rank_desc(imp, nsb) < NSA_SLC_TOPK, 1.0, 0.0).astype(BF16)

    init = (jnp.full((R, 1), NEG, F32), jnp.zeros((R, 1), F32), jnp.zeros((R, HEAD_DIM), F32))

    TS = 256
    jrow = lax.broadcasted_iota(jnp.int32, (128, TS), 0)
    kcol_s = lax.broadcasted_iota(jnp.int32, (1, TS), 1)
    jcol = lax.broadcasted_iota(jnp.int32, (128, TS), 1) // NSA_SLC_BLOCK

    def slc_body(n, carry):
        start = pl.multiple_of(n * TS, TS)
        kb = kk_ref[0, pl.ds(start, TS), 0:HEAD_DIM].astype(BF16)
        vb = vv_ref[0, pl.ds(start, TS), 0:HEAD_DIM].astype(BF16)
        sc = _dot_nt(q4, kb) * scale
        expand = jnp.where(jrow == n * (TS // NSA_SLC_BLOCK) + jcol, 1.0, 0.0).astype(BF16)
        selk = _dot(sel, expand)
        okq = jnp.logical_and(selk > 0.5, (start + kcol_s) <= tq)
        okf = jnp.where(okq, 1.0, 0.0)
        ok4 = jnp.concatenate([okf] * N_HEADS, axis=0) > 0.5
        return _softmax_step(carry, sc, ok4, vb)

    _, l_s, acc_s = lax.fori_loop(0, ci // (TS // Q_BLOCK) + 1, slc_body, init)
    o_slc = acc_s / l_s

    TW = Q_BLOCK
    kcol_w = lax.broadcasted_iota(jnp.int32, (1, TW), 1)

    def win_body(n, carry):
        start = pl.multiple_of(n * TW, TW)
        kb = kk_ref[0, pl.ds(start, TW), HEAD_DIM:2 * HEAD_DIM].astype(BF16)
        vb = vv_ref[0, pl.ds(start, TW), HEAD_DIM:2 * HEAD_DIM].astype(BF16)
        sc = _dot_nt(q4, kb) * scale
        kpos = start + kcol_w
        ok = jnp.logical_and(kpos <= tq4, kpos > tq4 - NSA_WINDOW)
        return _softmax_step(carry, sc, ok, vb)

    _, l_w, acc_w = lax.fori_loop(jnp.maximum(ci - NSA_WINDOW // TW, 0), ci + 1, win_body, init)
    o_win = acc_w / l_w

    gates = 1.0 / (1.0 + jnp.exp(-g_ref[0]))
    for h in range(N_HEADS):
        rs = slice(h * Q_BLOCK, (h + 1) * Q_BLOCK)
        o = (gates[:, 4 + h:5 + h] * o_cmp[rs] + gates[:, 8 + h:9 + h] * o_slc[rs]
             + gates[:, 12 + h:13 + h] * o_win[rs])
        o_ref[0, :, h * HEAD_DIM:(h + 1) * HEAD_DIM] = o


def _nsa_attend(y_nsa, y_small, kc, vc, ovl):
    b, s, _ = y_nsa.shape
    nq = s // Q_BLOCK
    return pl.pallas_call(
        _nsa_kernel,
        grid=(b, nq),
        in_specs=[pl.BlockSpec((1, Q_BLOCK, GROUP_WIDTH), lambda bi, ci: (bi, ci, 0)),
                  pl.BlockSpec((1, s, 128), lambda bi, ci: (bi, 0, 3)),
                  pl.BlockSpec((1, s, 128), lambda bi, ci: (bi, 0, 4)),
                  pl.BlockSpec((1, 128, HEAD_DIM), lambda bi, ci: (bi, 0, 0)),
                  pl.BlockSpec((1, 128, HEAD_DIM), lambda bi, ci: (bi, 0, 0)),
                  pl.BlockSpec((1, Q_BLOCK, W_SMALL), lambda bi, ci: (bi, ci, 0)),
                  pl.BlockSpec((128, 128), lambda bi, ci: (0, 0))],
        out_specs=pl.BlockSpec((1, Q_BLOCK, GROUP_WIDTH), lambda bi, ci: (bi, ci, 0)),
        out_shape=jax.ShapeDtypeStruct((b, s, GROUP_WIDTH), F32),
        compiler_params=_params(("parallel", "parallel")),
        name="nsa_attend",
    )(y_nsa, y_nsa, y_nsa, kc, vc, y_small, ovl)


def _outproj_kernel(x_ref, a_ref, b_ref, c_ref, d_ref, wo_ref, nw_ref, wq_ref, xo_ref, hn_ref, q_ref):
    acc = x_ref[...]
    for j, ref in enumerate((a_ref, b_ref, c_ref, d_ref)):
        acc = acc + _dot(ref[...].astype(BF16), wo_ref[j * GROUP_WIDTH:(j + 1) * GROUP_WIDTH, :])
    xo_ref[...] = acc
    hn = acc * lax.rsqrt(jnp.mean(acc * acc, axis=-1, keepdims=True) + NORM_EPS) * nw_ref[...]
    hn_ref[...] = hn
    q_ref[...] = _dot(hn.astype(BF16), wq_ref[...])


def _outproj(x2, mixed, wo, nw, wq):
    t, d = x2.shape
    nq = wq.shape[1]
    row = lambda wd: pl.BlockSpec((IN_TM, wd), lambda i: (i, 0))
    full = lambda shape: pl.BlockSpec(shape, lambda i: (0,) * len(shape))
    return pl.pallas_call(
        _outproj_kernel,
        grid=(t // IN_TM,),
        in_specs=[row(d)] + [row(GROUP_WIDTH)] * 4 + [full(wo.shape), full(nw.shape), full(wq.shape)],
        out_specs=[row(d), row(d), row(nq)],
        out_shape=[jax.ShapeDtypeStruct((t, d), F32), jax.ShapeDtypeStruct((t, d), F32),
                   jax.ShapeDtypeStruct((t, nq), F32)],
        compiler_params=_params(("parallel",)),
        name="outproj",
    )(x2, *mixed, wo, nw, wq)


def _topk_rows(v, k, payload=None):
    rows = lax.broadcasted_iota(jnp.int32, v.shape, 0).astype(F32)
    big = float(v.shape[0])
    vals, idxs = [], []
    for _ in range(k):
        m = jnp.max(v, axis=0, keepdims=True)
        first = jnp.min(jnp.where(v == m, rows, big), axis=0, keepdims=True)
        hit = rows == first
        vals.append(m)
        if payload is None:
            idxs.append(first)
        else:
            idxs.append(jnp.max(jnp.where(hit, payload, -1.0), axis=0, keepdims=True))
        v = jnp.where(hit, -jnp.inf, v)
    return jnp.concatenate(vals, axis=0), jnp.concatenate(idxs, axis=0)


_PAIR_RANGES = ((0, 24), (24, 29), (32, 36), (40, 43), (44, 46), (48, 50), (52, 54), (56, 64))


def _pruned_pairs(p1, p2, comb):
    lo = lax.broadcasted_iota(jnp.int32, (8, 128), 0) < 4
    p2a = p2[0:8]
    p2d = jnp.where(lo, p2a, pltpu.roll(p2a, 4, 0))
    blocks = [comb(p1[0:1], p2a), comb(p1[0:1], p2[8:16]), comb(p1[1:2], p2a), comb(p1[2:3], p2a),
              comb(p1[3:4], p2a), comb(jnp.where(lo, p1[4:5], p1[5:6]), p2d),
              comb(jnp.where(lo, p1[6:7], p1[7:8]), p2d), comb(p1[8:16], p2[0:1])]
    return jnp.concatenate(blocks, axis=0)


def _peer_topk_kernel(q_ref, keys_ref, idx_ref, gw_ref):
    for h in range(PEER_TOPK_HEADS):
        q = q_ref[:, 2 * PEER_NKEYS * h:2 * PEER_NKEYS * (h + 1)].astype(BF16)
        s1 = _dot_nt(keys_ref[h, 0].astype(BF16), q[:, :PEER_NKEYS])
        s2 = _dot_nt(keys_ref[h, 1].astype(BF16), q[:, PEER_NKEYS:])
        v1, i1 = _topk_rows(s1, PEER_TOPK)
        v2, i2 = _topk_rows(s2, PEER_TOPK)
        cand = _pruned_pairs(v1, v2, lambda a, b: a + b)
        cidx = _pruned_pairs(i1, i2, lambda a, b: a * float(PEER_NKEYS) + b)
        row = lax.broadcasted_iota(jnp.int32, cand.shape, 0)
        valid = functools.reduce(jnp.logical_or, [jnp.logical_and(row >= lo, row < hi) for lo, hi in _PAIR_RANGES])
        top_s, top_i = _topk_rows(jnp.where(valid, cand, -jnp.inf), PEER_TOPK, payload=cidx)
        e = jnp.exp(top_s - top_s[0:1, :])
        rows = slice(h * PEER_TOPK, (h + 1) * PEER_TOPK)
        gw_ref[0, rows, :] = e / jnp.sum(e, axis=0, keepdims=True)
        idx_ref[0, rows, :] = (top_i * 4.0).astype(jnp.int32)


def _peer_topk(q, keys):
    t = q.shape[0]
    nblk = t // 128
    hs = PEER_TOPK_HEADS
    shape = jax.ShapeDtypeStruct((nblk, PEER_PICKS, 128), jnp.int32)
    oblk = pl.BlockSpec((1, hs * PEER_TOPK, 128), lambda i, h: (i, h, 0))
    return pl.pallas_call(
        _peer_topk_kernel,
        grid=(nblk, PEER_HEADS // hs),
        in_specs=[pl.BlockSpec((128, hs * 2 * PEER_NKEYS), lambda i, h: (i, h)),
                  pl.BlockSpec((hs, 2, PEER_NKEYS, PEER_NKEYS), lambda i, h: (h, 0, 0, 0))],
        out_specs=[oblk, oblk],
        out_shape=[shape, jax.ShapeDtypeStruct(shape.shape, F32)],
        compiler_params=_params(("parallel", "parallel")),
        name="peer_topk",
    )(q, keys)


def _table_spec(table):
    return pl.BlockSpec(table.shape, lambda i: (0, 0), pipeline_mode=pl.Buffered(1))


def _unpack(words):
    lo = lax.bitcast_convert_type(lax.shift_left(words, jnp.uint32(16)), F32)
    hi = lax.bitcast_convert_type(jnp.bitwise_and(words, jnp.uint32(0xFFFF0000)), F32)
    return lo, hi


def _peer_act_kernel(idx_ref, x_ref, gw_ref, tab_ref, o_ref, prod_ref):
    rows_per_tok = 4 * PEER_PICKS

    def gather(t, _):
        xlo = x_ref[t, 0:4, :]
        xhi = x_ref[t, 4:8, :]
        base = pl.multiple_of(t * rows_per_tok, rows_per_tok)
        for k in range(PEER_PICKS):
            r = pl.multiple_of(idx_ref[t, k], 4)
            lo, hi = _unpack(tab_ref[pl.ds(r, 4), :])
            prod_ref[pl.ds(base + 4 * k, 4), :] = lo * xlo + hi * xhi
        return 0

    lax.fori_loop(0, PEER_TB, gather, 0)

    ones = jnp.ones((128, 128), BF16)
    own = (lax.broadcasted_iota(jnp.int32, (rows_per_tok, PEER_PICKS), 0) // 4
           == lax.broadcasted_iota(jnp.int32, (rows_per_tok, PEER_PICKS), 1))
    own = jnp.where(own, 1.0, 0.0)

    def tail(g, _):
        acts = []
        for tt in range(8):
            base = pl.multiple_of((g * 8 + tt) * rows_per_tok, rows_per_tok)
            sums = _dot(prod_ref[pl.ds(base, rows_per_tok), :].astype(BF16), ones)
            acts.append(jnp.sum(sums * own, axis=0, keepdims=True))
        r0 = pl.multiple_of(g * 8, 8)
        o_ref[pl.ds(r0, 8), :] = _gelu(jnp.concatenate(acts, axis=0)) * gw_ref[pl.ds(r0, 8), :]
        return 0

    lax.fori_loop(0, PEER_TB // 8, tail, 0)


def _peer_act(idx, x8, gw, table):
    t = idx.shape[0]
    return pl.pallas_call(
        _peer_act_kernel,
        grid=(t // PEER_TB,),
        in_specs=[pl.BlockSpec((PEER_TB, PEER_PICKS), lambda i: (i, 0), memory_space=pltpu.SMEM),
                  pl.BlockSpec((PEER_TB, 8, 128), lambda i: (i, 0, 0)),
                  pl.BlockSpec((PEER_TB, PEER_PICKS), lambda i: (i, 0)),
                  _table_spec(table)],
        out_specs=pl.BlockSpec((PEER_TB, PEER_PICKS), lambda i: (i, 0)),
        out_shape=jax.ShapeDtypeStruct((t, PEER_PICKS), F32),
        scratch_shapes=[pltpu.VMEM((PEER_TB * 4 * PEER_PICKS, 128), F32)],
        compiler_params=_params(("arbitrary",)),
        name="peer_act",
    )(idx, x8, gw, table)


def _peer_out_kernel(idx_ref, w_ref, x_ref, tab_ref, o_ref, rows_ref, wrep_ref):
    n = 4 * PEER_PICKS

    def gather(t, _):
        base = pl.multiple_of(t * n, n)
        for k in range(PEER_PICKS):
            r = pl.multiple_of(idx_ref[t, k], 4)
            rows_ref[pl.ds(base + 4 * k, 4), :] = tab_ref[pl.ds(r, 4), :]
        return 0

    lax.fori_loop(0, PEER_TB, gather, 0)

    rep = (lax.broadcasted_iota(jnp.int32, (PEER_PICKS, n), 1) // 4
           == lax.broadcasted_iota(jnp.int32, (PEER_PICKS, n), 0))
    wrep_ref[...] = _dot(w_ref[...].astype(BF16), jnp.where(rep, 1.0, 0.0).astype(BF16))
    pattern = (lax.broadcasted_iota(jnp.int32, (8, n), 1) % 4) == lax.broadcasted_iota(jnp.int32, (8, n), 0)

    def tail(g, _):
        for tt in range(4):
            t = g * 4 + tt
            lo, hi = _unpack(rows_ref[pl.ds(pl.multiple_of(t * n, n), n), :])
            sel = jnp.where(pattern, jnp.broadcast_to(wrep_ref[pl.ds(t, 1), :], (8, n)), 0.0).astype(BF16)
            out_lo = _dot(sel, lo.astype(BF16))
            out_hi = _dot(sel, hi.astype(BF16))
            o_ref[t] = x_ref[t] + jnp.concatenate([out_lo[0:4], out_hi[0:4]], axis=0)
        return 0

    lax.fori_loop(0, PEER_TB // 4, tail, 0)


def _peer_out(idx, w, x8, table):
    t = idx.shape[0]
    return pl.pallas_call(
        _peer_out_kernel,
        grid=(t // PEER_TB,),
        in_specs=[pl.BlockSpec((PEER_TB, PEER_PICKS), lambda i: (i, 0), memory_space=pltpu.SMEM),
                  pl.BlockSpec((PEER_TB, PEER_PICKS), lambda i: (i, 0)),
                  pl.BlockSpec((PEER_TB, 8, 128), lambda i: (i, 0, 0)),
                  _table_spec(table)],
        out_specs=pl.BlockSpec((PEER_TB, 8, 128), lambda i: (i, 0, 0)),
        out_shape=jax.ShapeDtypeStruct((t, 8, 128), F32),
        scratch_shapes=[pltpu.VMEM((PEER_TB * 4 * PEER_PICKS, 128), jnp.uint32),
                        pltpu.VMEM((PEER_TB, 4 * PEER_PICKS), F32)],
        compiler_params=_params(("arbitrary",)),
        name="peer_out",
    )(idx, w, x8, table)


def _pack_table(tab):
    e, d = tab.shape
    bits = lax.bitcast_convert_type(tab.astype(jnp.bfloat16), jnp.uint16).astype(jnp.uint32)
    words = bits[:, :d // 2] | (bits[:, d // 2:] << 16)
    return words.reshape(e * 4, 128)


def _norm_kernel(x_ref, w_ref, o_ref):
    x = x_ref[...]
    o_ref[...] = x * lax.rsqrt(jnp.mean(x * x, axis=-1, keepdims=True) + NORM_EPS) * w_ref[...]


def _final_norm(x2, w):
    t, d = x2.shape
    tm = 512
    return pl.pallas_call(
        _norm_kernel,
        grid=(t // tm,),
        in_specs=[pl.BlockSpec((tm, d), lambda i: (i, 0)), pl.BlockSpec((1, d), lambda i: (0, 0))],
        out_specs=pl.BlockSpec((tm, d), lambda i: (i, 0)),
        out_shape=jax.ShapeDtypeStruct((t, d), F32),
        compiler_params=_params(("parallel",)),
        name="final_norm",
    )(x2, w)


def _permute_w_in(w):
    cols = [w[:, :2816],
            w[:, 2820:3076],
            w[:, 3076:3140], w[:, 3140:3204],
            w[:, 3204:3268], w[:, 3332:3396],
            w[:, 3268:3332], w[:, 3396:3460],
            w[:, 2816:2820], w[:, 3460:3472],
            jnp.zeros((w.shape[0], W_SMALL - 16), w.dtype)]
    return jnp.concatenate(cols, axis=1).astype(BF16)


def _pad_lanes(v, n=128):
    return jnp.pad(v, (0, n - v.shape[0]))[None, :]


def kernel(x, norm_mix, w_in, ret_norm, ssm_conv_w, ssm_conv_b, ssm_dt_bias, ssm_a_log, ssm_d, ssm_norm, nsa_pe_k, nsa_w1_k, nsa_w2_k, nsa_pe_v, nsa_w1_v, nsa_w2_v, w_out, norm_ffn, peer_wq, peer_keys, peer_u, peer_v, norm_final):
    b, s, d = x.shape
    t = b * s
    depth = w_in.shape[0]
    rc, ra, rb = (jnp.asarray(a) for a in _rope_tables())
    intra, qd, kd, cd = (jnp.asarray(a) for a in _ret_tables())
    ovl = jnp.asarray(_nsa_overlap()).astype(BF16)
    tril = jnp.asarray(np.tril(np.ones((SSM_CHUNK, SSM_CHUNK), np.float32))).astype(BF16)

    for i in range(depth):
        y_ret, y_moba, y_ssm, y_nsa, y_small = _inproj(
            x, norm_mix[i][None, :], _permute_w_in(w_in[i]), rc, ra, rb)
        o_ret = _retention(y_ret, intra, qd, kd, cd, ret_norm[i][None, :])
        o_moba = _moba(y_moba)
        o_ssm = _ssd(y_ssm, y_small, ssm_conv_w[i], ssm_conv_b[i][None, :],
                     _pad_lanes(jnp.pad(ssm_dt_bias[i], (0, 0))), _pad_lanes(ssm_a_log[i]),
                     jnp.repeat(ssm_d[i], HEAD_DIM)[None, :], ssm_norm[i][None, :], tril)
        k16 = y_nsa[:, :, 256:320].reshape(b, s // NSA_CMP_STRIDE, NSA_CMP_STRIDE * HEAD_DIM)
        v16 = y_nsa[:, :, 320:384].reshape(b, s // NSA_CMP_STRIDE, NSA_CMP_STRIDE * HEAD_DIM)
        kc, vc = _nsa_compress(k16, v16,
                               nsa_pe_k[i].reshape(1, -1), nsa_w1_k[i].astype(BF16), nsa_w2_k[i].astype(BF16),
                               nsa_pe_v[i].reshape(1, -1), nsa_w1_v[i].astype(BF16), nsa_w2_v[i].astype(BF16))
        o_nsa = _nsa_attend(y_nsa, y_small, kc, vc, ovl)
        mixed = [o.reshape(t, GROUP_WIDTH) for o in (o_ret, o_moba, o_ssm, o_nsa)]
        x_mid, hn, q = _outproj(x.reshape(t, d), mixed, w_out[i].astype(BF16), norm_ffn[i][None, :],
                                peer_wq[i].astype(BF16))
        idx_t, gw_t = _peer_topk(q, peer_keys[i])
        idx = idx_t.transpose(0, 2, 1).reshape(t, PEER_PICKS)
        gw = gw_t.transpose(0, 2, 1).reshape(t, PEER_PICKS)
        wts = _peer_act(idx, hn.reshape(t, 8, 128), gw, _pack_table(peer_u[i]))
        x = _peer_out(idx, wts, x_mid.reshape(t, 8, 128), _pack_table(peer_v[i])).reshape(b, s, d)
    return _final_norm(x.reshape(t, d), norm_final[None, :]).reshape(b, s, d)
```

```python
import functools
import math

import numpy as np
import jax
import jax.numpy as jnp
from jax import lax
from jax.experimental import pallas as pl
from jax.experimental.pallas import tpu as pltpu

F32 = jnp.float32
BF16 = jnp.bfloat16

D_MODEL = 1024
SEQ = 2048
GROUP_WIDTH = 256
HEAD_DIM = 64
N_HEADS = 4
NORM_EPS = 1e-6
NEG = -1e30

ROPE_THETA = 500000.0
ROPE_DIMS = 16
RET_THETA = 10000.0
RET_CHUNK = 128
MOBA_BLOCK = 256
MOBA_TOPK = 3
SSM_CHUNK = 256
SSM_STATE = 128
SSM_CONV = 4
NSA_CMP_BLOCK = 32
NSA_CMP_STRIDE = 16
NSA_SLC_BLOCK = 64
NSA_SLC_TOPK = 16
NSA_WINDOW = 512
NSA_FORCE_BONUS = 1e6
PEER_HEADS = 8
PEER_NKEYS = 128
PEER_TOPK = 16
PEER_PICKS = PEER_HEADS * PEER_TOPK
PEER_TOPK_HEADS = 4
Q_BLOCK = 128

W_RET, W_MOBA, W_SSM, W_NSA, W_SMALL = 1024, 768, 1024, 640, 128
IN_COLS_PADDED = W_RET + W_MOBA + W_SSM + W_NSA + W_SMALL
ROPE_W = 512

IN_TM = 256
PEER_TB = 32
VMEM_LIMIT = 56 * 1024 * 1024


def _params(sem, vmem=VMEM_LIMIT):
    return pltpu.CompilerParams(dimension_semantics=sem, vmem_limit_bytes=vmem)


def _dot(a, b):
    return jnp.dot(a, b, preferred_element_type=F32)


def _dot_nt(a, b):
    return lax.dot_general(a, b, (((1,), (1,)), ((), ())), preferred_element_type=F32)


def _dot_tn(a, b):
    return lax.dot_general(a, b, (((0,), (0,)), ((), ())), preferred_element_type=F32)


def _split_bf16(x):
    hi = x.astype(BF16)
    lo = (x - hi.astype(F32)).astype(BF16)
    return hi, lo


def _silu(x):
    return x * (1.0 / (1.0 + jnp.exp(-x)))


def _gelu(x):
    return 0.5 * x * (1.0 + lax.erf(x * (1.0 / math.sqrt(2.0))))


def _rope_region(seq, theta, rot_dims, n_blocks, scale_blocks=(), ident_blocks=()):
    half = rot_dims // 2
    inv = 1.0 / (theta ** (np.arange(0, rot_dims, 2, dtype=np.float64) / rot_dims))
    ang = np.arange(seq, dtype=np.float64)[:, None] * inv[None, :]
    cos, sin = np.cos(ang), np.sin(ang)
    c = np.ones((seq, HEAD_DIM)); a = np.zeros((seq, HEAD_DIM)); b = np.zeros((seq, HEAD_DIM))
    c[:, :half] = cos; c[:, half:rot_dims] = cos
    a[:, :half] = -sin
    b[:, half:rot_dims] = sin
    cs, as_, bs = [], [], []
    for blk in range(n_blocks):
        if blk in ident_blocks:
            cs.append(np.ones_like(c)); as_.append(np.zeros_like(a)); bs.append(np.zeros_like(b))
            continue
        f = scale_blocks.get(blk, 1.0) if isinstance(scale_blocks, dict) else 1.0
        cs.append(c * f); as_.append(a * f); bs.append(b * f)
    return np.concatenate(cs, 1), np.concatenate(as_, 1), np.concatenate(bs, 1)


@functools.lru_cache(maxsize=None)
def _rope_tables():
    kscale = HEAD_DIM ** -0.5
    ret = _rope_region(SEQ, RET_THETA, HEAD_DIM, 8, scale_blocks={4: kscale, 5: kscale, 6: kscale, 7: kscale})
    moba = _rope_region(SEQ, ROPE_THETA, ROPE_DIMS, 8)
    nsa = _rope_region(SEQ, ROPE_THETA, ROPE_DIMS, 8, ident_blocks=(5,))
    return tuple(np.concatenate([ret[i], moba[i], nsa[i]], 1).astype(np.float32) for i in range(3))


@functools.lru_cache(maxsize=None)
def _ret_tables():
    h, c = N_HEADS, RET_CHUNK
    log_gamma = np.log1p(-np.exp2(-5.0 - np.arange(h, dtype=np.float64)))
    pos = np.arange(c, dtype=np.float64)
    diff = pos[:, None] - pos[None, :]
    intra = np.where(diff >= 0, np.exp(log_gamma[:, None, None] * np.maximum(diff, 0.0)), 0.0)
    q_dec = np.exp(log_gamma[:, None] * (pos + 1.0))
    k_dec = np.exp(log_gamma[:, None] * (c - 1.0 - pos))
    c_dec = np.exp(log_gamma * c)
    qd = np.repeat(q_dec.T, HEAD_DIM, axis=1)
    kd = np.repeat(k_dec.T, HEAD_DIM, axis=1)
    cd = np.broadcast_to(c_dec[:, None, None], (h, HEAD_DIM, HEAD_DIM))
    return (intra.astype(np.float32), qd.astype(np.float32), kd.astype(np.float32),
            np.ascontiguousarray(cd).astype(np.float32))


@functools.lru_cache(maxsize=None)
def _nsa_overlap():
    nc = (SEQ - NSA_CMP_BLOCK) // NSA_CMP_STRIDE + 1
    starts = np.arange(128) * NSA_CMP_STRIDE
    sb = np.arange(128)
    ov = ((starts[:, None] < (sb[None, :] + 1) * NSA_SLC_BLOCK)
          & ((starts + NSA_CMP_BLOCK)[:, None] > sb[None, :] * NSA_SLC_BLOCK))
    ov = ov & (np.arange(128)[:, None] < nc) & (sb[None, :] < SEQ // NSA_SLC_BLOCK)
    return ov.astype(np.float32)


def _inproj_kernel(x_ref, nw_ref, w_ref, rc_ref, ra_ref, rb_ref,
                   o_ret, o_moba, o_ssm, o_nsa, o_small):
    x = x_ref[0]
    ms = jnp.mean(x * x, axis=-1, keepdims=True)
    h = (x * lax.rsqrt(ms + NORM_EPS) * nw_ref[...]).astype(BF16)

    def rope(y, region, shift):
        sl = slice(region * ROPE_W, (region + 1) * ROPE_W)
        return (y * rc_ref[:, sl] + pltpu.roll(y, ROPE_W - shift, 1) * ra_ref[:, sl]
                + pltpu.roll(y, shift, 1) * rb_ref[:, sl])

    off = 0
    y = _dot(h, w_ref[:, off:off + W_RET]); off += W_RET
    o_ret[0, :, :ROPE_W] = rope(y[:, :ROPE_W], 0, HEAD_DIM // 2)
    o_ret[0, :, ROPE_W:] = y[:, ROPE_W:]
    y = _dot(h, w_ref[:, off:off + W_MOBA]); off += W_MOBA
    o_moba[0, :, :ROPE_W] = rope(y[:, :ROPE_W], 1, ROPE_DIMS // 2)
    o_moba[0, :, ROPE_W:] = y[:, ROPE_W:]
    o_ssm[0] = _dot(h, w_ref[:, off:off + W_SSM]); off += W_SSM
    y = _dot(h, w_ref[:, off:off + W_NSA]); off += W_NSA
    o_nsa[0, :, :ROPE_W] = rope(y[:, :ROPE_W], 2, ROPE_DIMS // 2)
    o_nsa[0, :, ROPE_W:] = y[:, ROPE_W:]
    o_small[0] = _dot(h, w_ref[:, off:off + W_SMALL])


def _inproj(x, nw, w, rc, ra, rb):
    b, s, d = x.shape
    ns = s // IN_TM
    widths = (W_RET, W_MOBA, W_SSM, W_NSA, W_SMALL)
    tab = pl.BlockSpec((IN_TM, 3 * ROPE_W), lambda si, bi: (si, 0))
    return pl.pallas_call(
        _inproj_kernel,
        grid=(ns, b),
        in_specs=[pl.BlockSpec((1, IN_TM, d), lambda si, bi: (bi, si, 0)),
                  pl.BlockSpec((1, d), lambda si, bi: (0, 0)),
                  pl.BlockSpec((d, IN_COLS_PADDED), lambda si, bi: (0, 0)),
                  tab, tab, tab],
        out_specs=[pl.BlockSpec((1, IN_TM, wd), lambda si, bi: (bi, si, 0)) for wd in widths],
        out_shape=[jax.ShapeDtypeStruct((b, s, wd), F32) for wd in widths],
        compiler_params=_params(("parallel", "parallel")),
        name="inproj",
    )(x, nw, w, rc, ra, rb)


def _ret_kernel(y_ref, intra_ref, qd_ref, kd_ref, cd_ref, nw_ref, o_ref, state_ref):
    @pl.when(pl.program_id(1) == 0)
    def _():
        state_ref[...] = jnp.zeros_like(state_ref)

    for h in range(N_HEADS):
        sl = slice(h * HEAD_DIM, (h + 1) * HEAD_DIM)
        q = y_ref[0, :, h * HEAD_DIM:(h + 1) * HEAD_DIM].astype(BF16)
        k = y_ref[0, :, GROUP_WIDTH + h * HEAD_DIM:GROUP_WIDTH + (h + 1) * HEAD_DIM]
        v = y_ref[0, :, 2 * GROUP_WIDTH + h * HEAD_DIM:2 * GROUP_WIDTH + (h + 1) * HEAD_DIM].astype(BF16)
        g = y_ref[0, :, 3 * GROUP_WIDTH + h * HEAD_DIM:3 * GROUP_WIDTH + (h + 1) * HEAD_DIM]
        att = _dot_nt(q, k.astype(BF16)) * intra_ref[h]
        o_in = _dot(att.astype(BF16), v)
        st = state_ref[h]
        o_x = _dot(q, st.astype(BF16)) * qd_ref[:, sl]
        kv = _dot_tn((k * kd_ref[:, sl]).astype(BF16), v)
        state_ref[h] = st * cd_ref[h] + kv
        o = o_in + o_x
        o = o * lax.rsqrt(jnp.mean(o * o, axis=-1, keepdims=True) + NORM_EPS) * nw_ref[:, sl]
        o_ref[0, :, sl] = _silu(g) * o


def _retention(y_ret, intra, qd, kd, cd, nw):
    b, s, _ = y_ret.shape
    nch = s // RET_CHUNK
    full = lambda shape: pl.BlockSpec(shape, lambda bi, ci: (0,) * len(shape))
    return pl.pallas_call(
        _ret_kernel,
        grid=(b, nch),
        in_specs=[pl.BlockSpec((1, RET_CHUNK, W_RET), lambda bi, ci: (bi, ci, 0)),
                  full(intra.shape), full(qd.shape), full(kd.shape), full(cd.shape), full(nw.shape)],
        out_specs=pl.BlockSpec((1, RET_CHUNK, GROUP_WIDTH), lambda bi, ci: (bi, ci, 0)),
        out_shape=jax.ShapeDtypeStruct((b, s, GROUP_WIDTH), F32),
        scratch_shapes=[pltpu.VMEM((N_HEADS, HEAD_DIM, HEAD_DIM), F32)],
        compiler_params=_params(("parallel", "arbitrary")),
        name="retention",
    )(y_ret, intra, qd, kd, cd, nw)


def _softmax_step(carry, s, ok, vb):
    m, l, acc = carry
    s = jnp.where(ok, s, NEG)
    m_new = jnp.maximum(m, jnp.max(s, axis=1, keepdims=True))
    p = jnp.exp(s - m_new)
    alpha = jnp.exp(m - m_new)
    l = alpha * l + jnp.sum(p, axis=1, keepdims=True)
    acc = alpha * acc + _dot(p.astype(BF16), vb)
    return m_new, l, acc


def _rank_rows(vals, n_cand):
    row = lax.broadcasted_iota(jnp.int32, vals.shape, 0)
    rank = jnp.zeros(vals.shape, F32)
    for m in range(n_cand):
        cand = vals[m:m + 1, :]
        beats = jnp.logical_or(cand > vals, jnp.logical_and(cand == vals, row > m))
        rank = rank + jnp.where(beats, 1.0, 0.0)
    return rank


MOBA_GATE_ROWS = 16


def _moba_kernel(q_ref, k_ref, v_ref, o_ref, km_ref):
    ci = pl.program_id(1)
    nb = SEQ // MOBA_BLOCK

    @pl.when(ci == 0)
    def _():
        km_ref[...] = jnp.zeros_like(km_ref)
        for n in range(nb):
            km_ref[n:n + 1, :] = jnp.mean(k_ref[0, n * MOBA_BLOCK:(n + 1) * MOBA_BLOCK, :], axis=0, keepdims=True)

    cur = ci // (MOBA_BLOCK // Q_BLOCK)
    t0 = ci * Q_BLOCK
    tpos = t0 + lax.broadcasted_iota(jnp.int32, (Q_BLOCK, 1), 0)
    blk = lax.broadcasted_iota(jnp.int32, (MOBA_GATE_ROWS, Q_BLOCK), 0)
    lane = lax.broadcasted_iota(jnp.int32, (Q_BLOCK, 128), 1)
    eye = jnp.where(lax.broadcasted_iota(jnp.int32, (MOBA_GATE_ROWS, 128), 0)
                    == lax.broadcasted_iota(jnp.int32, (MOBA_GATE_ROWS, 128), 1), 1.0, 0.0).astype(BF16)
    kcol = lax.broadcasted_iota(jnp.int32, (1, MOBA_BLOCK), 1)
    scale = HEAD_DIM ** -0.5

    heads = [slice(h * HEAD_DIM, (h + 1) * HEAD_DIM) for h in range(N_HEADS)]
    qs, allows = [], []
    for sl in heads:
        q = (q_ref[0, :, sl] * scale).astype(BF16)
        gate = _dot_nt(km_ref[:, sl].astype(BF16), q)
        gate = jnp.where(blk < cur, gate, NEG)
        rank = _rank_rows(gate, nb)
        allow = jnp.logical_or(jnp.logical_and(rank < MOBA_TOPK, blk < cur), blk == cur)
        qs.append(q)
        allows.append(_dot_tn(jnp.where(allow, 1.0, 0.0).astype(BF16), eye))

    def body(n, carry):
        start = pl.multiple_of(n * MOBA_BLOCK, MOBA_BLOCK)
        causal = (start + kcol) <= tpos
        out = []
        for sl, q, allow, (m, l, acc) in zip(heads, qs, allows, carry):
            kb = k_ref[0, pl.ds(start, MOBA_BLOCK), sl].astype(BF16)
            vb = v_ref[0, pl.ds(start, MOBA_BLOCK), sl].astype(BF16)
            acol = jnp.sum(jnp.where(lane == n, allow, 0.0), axis=1, keepdims=True)
            ok = jnp.logical_and(acol > 0.5, causal)
            out.append(_softmax_step((m, l, acc), _dot_nt(q, kb), ok, vb))
        return tuple(out)

    init = (jnp.full((Q_BLOCK, 1), NEG, F32), jnp.zeros((Q_BLOCK, 1), F32),
            jnp.zeros((Q_BLOCK, HEAD_DIM), F32))
    final = lax.fori_loop(0, cur + 1, body, (init,) * N_HEADS)
    for sl, (m, l, acc) in zip(heads, final):
        o_ref[0, :, sl] = acc / l


def _moba(y_moba):
    b, s, _ = y_moba.shape
    nq = s // Q_BLOCK
    return pl.pallas_call(
        _moba_kernel,
        grid=(b, nq),
        in_specs=[pl.BlockSpec((1, Q_BLOCK, GROUP_WIDTH), lambda bi, ci: (bi, ci, 0)),
                  pl.BlockSpec((1, s, GROUP_WIDTH), lambda bi, ci: (bi, 0, 1)),
                  pl.BlockSpec((1, s, GROUP_WIDTH), lambda bi, ci: (bi, 0, 2))],
        out_specs=pl.BlockSpec((1, Q_BLOCK, GROUP_WIDTH), lambda bi, ci: (bi, ci, 0)),
        out_shape=jax.ShapeDtypeStruct((b, s, GROUP_WIDTH), F32),
        scratch_shapes=[pltpu.VMEM((MOBA_GATE_ROWS, GROUP_WIDTH), F32)],
        compiler_params=_params(("parallel", "arbitrary")),
        name="moba",
    )(y_moba, y_moba, y_moba)


def _ssd_kernel(z_ref, x_ref, b_ref, c_ref, dt_ref, cw_ref, cb_ref, dtb_ref, alog_ref,
                dskip_ref, nw_ref, tril_ref, o_ref, buf_ref, state_ref):
    L = SSM_CHUNK
    ci = pl.program_id(1)

    @pl.when(ci == 0)
    def _():
        buf_ref[...] = jnp.zeros_like(buf_ref)
        state_ref[...] = jnp.zeros_like(state_ref)

    convd = []
    for j, ref in enumerate((x_ref, b_ref, c_ref)):
        cs = slice(j * GROUP_WIDTH, (j + 1) * GROUP_WIDTH)
        buf_ref[8:8 + L, cs] = ref[0]
        acc = cb_ref[:, cs] + cw_ref[SSM_CONV - 1:SSM_CONV, cs] * ref[0]
        for i in range(SSM_CONV - 1):
            shift = SSM_CONV - 1 - i
            acc = acc + cw_ref[i:i + 1, cs] * buf_ref[8 - shift:8 - shift + L, cs]
        convd.append(_silu(acc))
    buf_ref[0:8, :] = buf_ref[L:L + 8, :]
    xs, bm, cm = convd

    dtr = dt_ref[0] + dtb_ref[...]
    dt = jnp.maximum(dtr, 0.0) + jnp.log1p(jnp.exp(-jnp.abs(dtr)))
    a = -jnp.exp(alog_ref[...])
    da = dt * a
    d1, d2, d3 = da.astype(BF16), None, None
    r1 = da - d1.astype(F32)
    d2 = r1.astype(BF16)
    d3 = (r1 - d2.astype(F32)).astype(BF16)
    tril = tril_ref[...]
    acol = _dot(tril, d1) + _dot(tril, d2) + _dot(tril, d3)
    arow = acol.T
    alast = acol[L - 1:L, :]
    row = lax.broadcasted_iota(jnp.int32, (L, L), 0)
    col = lax.broadcasted_iota(jnp.int32, (L, L), 1)
    causal = row >= col

    gmats = []
    for g in range(2):
        gs = slice(g * SSM_STATE, (g + 1) * SSM_STATE)
        gmats.append(_dot_nt(cm[:, gs].astype(BF16), bm[:, gs].astype(BF16)))

    for h in range(N_HEADS):
        sl = slice(h * HEAD_DIM, (h + 1) * HEAD_DIM)
        g = h // 2
        gs = slice(g * SSM_STATE, (g + 1) * SSM_STATE)
        a_c = acol[:, h:h + 1]
        a_r = arow[h:h + 1, :]
        a_l = alast[:, h:h + 1]
        decay = jnp.where(causal, jnp.exp(jnp.where(causal, a_c - a_r, 0.0)), 0.0)
        xh = xs[:, sl]
        xdt = xh * dt[:, h:h + 1]
        y_diag = _dot((gmats[g] * decay).astype(BF16), xdt.astype(BF16))
        to_end = jnp.exp(a_l - a_c)
        st_new = _dot_tn((xdt * to_end).astype(BF16), bm[:, gs].astype(BF16))
        prev = state_ref[h]
        y_off = _dot_nt(cm[:, gs].astype(BF16), prev.astype(BF16)) * jnp.exp(a_c)
        state_ref[h] = prev * jnp.exp(a_l) + st_new
        o_ref[0, :, sl] = y_diag + y_off + xh * dskip_ref[:, sl]

    y = o_ref[0] * _silu(z_ref[0])
    for g in range(2):
        gs = slice(g * 128, (g + 1) * 128)
        yg = y[:, gs]
        o_ref[0, :, gs] = yg * lax.rsqrt(jnp.mean(yg * yg, axis=-1, keepdims=True) + NORM_EPS) * nw_ref[:, gs]


def _ssd(y_ssm, y_small, cw, cb, dtb, alog, dskip, nw, tril):
    b, s, _ = y_ssm.shape
    nch = s // SSM_CHUNK
    L = SSM_CHUNK
    full = lambda shape: pl.BlockSpec(shape, lambda bi, ci: (0,) * len(shape))
    colblk = lambda j: pl.BlockSpec((1, L, GROUP_WIDTH), lambda bi, ci, j=j: (bi, ci, j))
    return pl.pallas_call(
        _ssd_kernel,
        grid=(b, nch),
        in_specs=[colblk(0), colblk(1), colblk(2), colblk(3),
                  pl.BlockSpec((1, L, W_SMALL), lambda bi, ci: (bi, ci, 0)),
                  full(cw.shape), full(cb.shape), full(dtb.shape), full(alog.shape),
                  full(dskip.shape), full(nw.shape), full(tril.shape)],
        out_specs=pl.BlockSpec((1, L, GROUP_WIDTH), lambda bi, ci: (bi, ci, 0)),
        out_shape=jax.ShapeDtypeStruct((b, s, GROUP_WIDTH), F32),
        scratch_shapes=[pltpu.VMEM((L + 8, 3 * GROUP_WIDTH), F32),
                        pltpu.VMEM((N_HEADS, HEAD_DIM, SSM_STATE), F32)],
        compiler_params=_params(("parallel", "arbitrary")),
        name="ssd",
    )(y_ssm, y_ssm, y_ssm, y_ssm, y_small, cw, cb, dtb, alog, dskip, nw, tril)


def _nsa_cmp_kernel(k_ref, v_ref, pek_ref, w1k_ref, w2k_ref, pev_ref, w1v_ref, w2v_ref, kc_ref, vc_ref):
    half = NSA_CMP_STRIDE * HEAD_DIM
    for x_ref, pe_ref, w1_ref, w2_ref, o_ref in ((k_ref, pek_ref, w1k_ref, w2k_ref, kc_ref),
                                                 (v_ref, pev_ref, w1v_ref, w2v_ref, vc_ref)):
        x = x_ref[0]
        a = _dot((x + pe_ref[:, :half]).astype(BF16), w1_ref[:half, :])
        bb = _dot((x + pe_ref[:, half:]).astype(BF16), w1_ref[half:, :])
        h1 = a + pltpu.roll(bb, bb.shape[0] - 1, 0)
        o_ref[0] = _dot(_gelu(h1).astype(BF16), w2_ref[...])


def _nsa_compress(k16, v16, pek, w1k, w2k, pev, w1v, w2v):
    b = k16.shape[0]
    full = lambda shape: pl.BlockSpec(shape, lambda bi: (0,) * len(shape))
    blk = pl.BlockSpec((1, 128, NSA_CMP_STRIDE * HEAD_DIM), lambda bi: (bi, 0, 0))
    oblk = pl.BlockSpec((1, 128, HEAD_DIM), lambda bi: (bi, 0, 0))
    return pl.pallas_call(
        _nsa_cmp_kernel,
        grid=(b,),
        in_specs=[blk, blk, full(pek.shape), full(w1k.shape), full(w2k.shape),
                  full(pev.shape), full(w1v.shape), full(w2v.shape)],
        out_specs=[oblk, oblk],
        out_shape=[jax.ShapeDtypeStruct((b, 128, HEAD_DIM), F32)] * 2,
        compiler_params=_params(("parallel",)),
        name="nsa_compress",
    )(k16, v16, pek, w1k, w2k, pev, w1v, w2v)


def _nsa_kernel(q_ref, kk_ref, vv_ref, kc_ref, vc_ref, g_ref, ovlt_ref, o_ref):
    ci = pl.program_id(1)
    t0 = ci * Q_BLOCK
    scale = HEAD_DIM ** -0.5
    R = N_HEADS * Q_BLOCK
    q4 = jnp.concatenate([q_ref[0, :, h * HEAD_DIM:(h + 1) * HEAD_DIM] for h in range(N_HEADS)], axis=0)
    q4 = (q4 * scale).astype(BF16)
    tq = t0 + lax.broadcasted_iota(jnp.int32, (Q_BLOCK, 1), 0)
    tq4 = jnp.concatenate([tq] * N_HEADS, axis=0)
    lane4 = lax.broadcasted_iota(jnp.int32, (R, 128), 1)

    s = _dot_nt(q4, kc_ref[0].astype(BF16))
    ok = (lane4 * NSA_CMP_STRIDE + (NSA_CMP_BLOCK - 1)) <= tq4
    s = jnp.where(ok, s, NEG)
    e = jnp.where(ok, jnp.exp(s - jnp.max(s, axis=1, keepdims=True)), 0.0)
    l = jnp.sum(e, axis=1, keepdims=True)
    p_cmp = e / jnp.where(l > 0.0, l, 1.0)
    pb = p_cmp.astype(BF16)
    o_cmp = _dot(pb, vc_ref[0].astype(BF16))

    nsb = SEQ // NSA_SLC_BLOCK
    imp = sum(_dot_nt(ovlt_ref[...], pb[h * Q_BLOCK:(h + 1) * Q_BLOCK]) for h in range(N_HEADS))[0:nsb]
    sblk = lax.broadcasted_iota(jnp.int32, (nsb, Q_BLOCK), 0)
    cur_b = (t0 + lax.broadcasted_iota(jnp.int32, (1, Q_BLOCK), 1)) // NSA_SLC_BLOCK
    forced = jnp.logical_or(jnp.logical_or(sblk == 0, sblk == cur_b), sblk == cur_b - 1)
    imp = jnp.where(forced, imp + NSA_FORCE_BONUS, imp)
    imp = jnp.where(sblk <= cur_b, imp, NEG)
    sel = jnp.where(_rank_rows(imp, nsb) < NSA_SLC_TOPK, 1.0, 0.0).astype(BF16)

    init = (jnp.full((R, 1), NEG, F32), jnp.zeros((R, 1), F32), jnp.zeros((R, HEAD_DIM), F32))

    TS = 256
    jrow = lax.broadcasted_iota(jnp.int32, (nsb, TS), 0)
    kcol_s = lax.broadcasted_iota(jnp.int32, (1, TS), 1)
    jcol = lax.broadcasted_iota(jnp.int32, (nsb, TS), 1) // NSA_SLC_BLOCK

    def slc_body(n, carry):
        start = pl.multiple_of(n * TS, TS)
        kb = kk_ref[0, pl.ds(start, TS), 0:HEAD_DIM].astype(BF16)
        vb = vv_ref[0, pl.ds(start, TS), 0:HEAD_DIM].astype(BF16)
        expand = jnp.where(jrow == n * (TS // NSA_SLC_BLOCK) + jcol, 1.0, 0.0).astype(BF16)
        selk = _dot_tn(sel, expand)
        okq = jnp.logical_and(selk > 0.5, (start + kcol_s) <= tq)
        okf = jnp.where(okq, 1.0, 0.0)
        ok4 = jnp.concatenate([okf] * N_HEADS, axis=0) > 0.5
        return _softmax_step(carry, _dot_nt(q4, kb), ok4, vb)

    _, l_s, acc_s = lax.fori_loop(0, ci // (TS // Q_BLOCK) + 1, slc_body, init)
    o_slc = acc_s / l_s

    TW = Q_BLOCK
    kcol_w = lax.broadcasted_iota(jnp.int32, (1, TW), 1)

    def win_body(n, carry):
        start = pl.multiple_of(n * TW, TW)
        kb = kk_ref[0, pl.ds(start, TW), HEAD_DIM:2 * HEAD_DIM].astype(BF16)
        vb = vv_ref[0, pl.ds(start, TW), HEAD_DIM:2 * HEAD_DIM].astype(BF16)
        kpos = start + kcol_w
        ok = jnp.logical_and(kpos <= tq4, kpos > tq4 - NSA_WINDOW)
        return _softmax_step(carry, _dot_nt(q4, kb), ok, vb)

    _, l_w, acc_w = lax.fori_loop(jnp.maximum(ci - NSA_WINDOW // TW, 0), ci + 1, win_body, init)
    o_win = acc_w / l_w

    gates = 1.0 / (1.0 + jnp.exp(-g_ref[0]))
    for h in range(N_HEADS):
        rs = slice(h * Q_BLOCK, (h + 1) * Q_BLOCK)
        o = (gates[:, 4 + h:5 + h] * o_cmp[rs] + gates[:, 8 + h:9 + h] * o_slc[rs]
             + gates[:, 12 + h:13 + h] * o_win[rs])
        o_ref[0, :, h * HEAD_DIM:(h + 1) * HEAD_DIM] = o


def _nsa_attend(y_nsa, y_small, kc, vc, ovlt):
    b, s, _ = y_nsa.shape
    nq = s // Q_BLOCK
    return pl.pallas_call(
        _nsa_kernel,
        grid=(b, nq),
        in_specs=[pl.BlockSpec((1, Q_BLOCK, GROUP_WIDTH), lambda bi, ci: (bi, ci, 0)),
                  pl.BlockSpec((1, s, 128), lambda bi, ci: (bi, 0, 3)),
                  pl.BlockSpec((1, s, 128), lambda bi, ci: (bi, 0, 4)),
                  pl.BlockSpec((1, 128, HEAD_DIM), lambda bi, ci: (bi, 0, 0)),
                  pl.BlockSpec((1, 128, HEAD_DIM), lambda bi, ci: (bi, 0, 0)),
                  pl.BlockSpec((1, Q_BLOCK, W_SMALL), lambda bi, ci: (bi, ci, 0)),
                  pl.BlockSpec((128, 128), lambda bi, ci: (0, 0))],
        out_specs=pl.BlockSpec((1, Q_BLOCK, GROUP_WIDTH), lambda bi, ci: (bi, ci, 0)),
        out_shape=jax.ShapeDtypeStruct((b, s, GROUP_WIDTH), F32),
        compiler_params=_params(("parallel", "parallel")),
        name="nsa_attend",
    )(y_nsa, y_nsa, y_nsa, kc, vc, y_small, ovlt)


def _outproj_kernel(x_ref, a_ref, b_ref, c_ref, d_ref, wo_ref, nw_ref, wq_ref, xo_ref, hn_ref, q_ref):
    acc = x_ref[...]
    for j, ref in enumerate((a_ref, b_ref, c_ref, d_ref)):
        acc = acc + _dot(ref[...].astype(BF16), wo_ref[j * GROUP_WIDTH:(j + 1) * GROUP_WIDTH, :])
    xo_ref[...] = acc
    hn = acc * lax.rsqrt(jnp.mean(acc * acc, axis=-1, keepdims=True) + NORM_EPS) * nw_ref[...]
    hn_ref[...] = hn
    q_ref[...] = _dot(hn.astype(BF16), wq_ref[...])


def _outproj(x2, mixed, wo, nw, wq):
    t, d = x2.shape
    nq = wq.shape[1]
    row = lambda wd: pl.BlockSpec((IN_TM, wd), lambda i: (i, 0))
    full = lambda shape: pl.BlockSpec(shape, lambda i: (0,) * len(shape))
    return pl.pallas_call(
        _outproj_kernel,
        grid=(t // IN_TM,),
        in_specs=[row(d)] + [row(GROUP_WIDTH)] * 4 + [full(wo.shape), full(nw.shape), full(wq.shape)],
        out_specs=[row(d), row(d), row(nq)],
        out_shape=[jax.ShapeDtypeStruct((t, d), F32), jax.ShapeDtypeStruct((t, d), F32),
                   jax.ShapeDtypeStruct((t, nq), F32)],
        compiler_params=_params(("parallel",)),
        name="outproj",
    )(x2, *mixed, wo, nw, wq)


def _topk_rows(v, k, payload=None):
    rows = lax.broadcasted_iota(jnp.int32, v.shape, 0).astype(F32)
    big = float(v.shape[0])
    vals, idxs = [], []
    for _ in range(k):
        m = jnp.max(v, axis=0, keepdims=True)
        first = jnp.min(jnp.where(v == m, rows, big), axis=0, keepdims=True)
        hit = rows == first
        vals.append(m)
        if payload is None:
            idxs.append(first)
        else:
            idxs.append(jnp.max(jnp.where(hit, payload, -1.0), axis=0, keepdims=True))
        v = jnp.where(hit, -jnp.inf, v)
    return jnp.concatenate(vals, axis=0), jnp.concatenate(idxs, axis=0)


_PAIR_RANGES = ((0, 24), (24, 29), (32, 36), (40, 43), (44, 46), (48, 50), (52, 54), (56, 64))


def _pruned_pairs(p1, p2, comb):
    lo = lax.broadcasted_iota(jnp.int32, (8, 128), 0) < 4
    p2a = p2[0:8]
    p2d = jnp.where(lo, p2a, pltpu.roll(p2a, 4, 0))
    blocks = [comb(p1[0:1], p2a), comb(p1[0:1], p2[8:16]), comb(p1[1:2], p2a), comb(p1[2:3], p2a),
              comb(p1[3:4], p2a), comb(jnp.where(lo, p1[4:5], p1[5:6]), p2d),
              comb(jnp.where(lo, p1[6:7], p1[7:8]), p2d), comb(p1[8:16], p2[0:1])]
    return jnp.concatenate(blocks, axis=0)


def _peer_topk_kernel(q_ref, keys_ref, idx_ref, gw_ref):
    for h in range(PEER_TOPK_HEADS):
        q = q_ref[:, 2 * PEER_NKEYS * h:2 * PEER_NKEYS * (h + 1)].astype(BF16)
        s1 = _dot_nt(keys_ref[h, 0].astype(BF16), q[:, :PEER_NKEYS])
        s2 = _dot_nt(keys_ref[h, 1].astype(BF16), q[:, PEER_NKEYS:])
        v1, i1 = _topk_rows(s1, PEER_TOPK)
        v2, i2 = _topk_rows(s2, PEER_TOPK)
        cand = _pruned_pairs(v1, v2, lambda a, b: a + b)
        cidx = _pruned_pairs(i1, i2, lambda a, b: a * float(PEER_NKEYS) + b)
        row = lax.broadcasted_iota(jnp.int32, cand.shape, 0)
        valid = functools.reduce(jnp.logical_or, [jnp.logical_and(row >= lo, row < hi) for lo, hi in _PAIR_RANGES])
        top_s, top_i = _topk_rows(jnp.where(valid, cand, -jnp.inf), PEER_TOPK, payload=cidx)
        e = jnp.exp(top_s - top_s[0:1, :])
        rows = slice(h * PEER_TOPK, (h + 1) * PEER_TOPK)
        gw_ref[0, rows, :] = e / jnp.sum(e, axis=0, keepdims=True)
        idx_ref[0, rows, :] = (top_i * 4.0).astype(jnp.int32)


def _peer_topk(q, keys):
    t = q.shape[0]
    nblk = t // 128
    hs = PEER_TOPK_HEADS
    shape = jax.ShapeDtypeStruct((nblk, PEER_PICKS, 128), jnp.int32)
    oblk = pl.BlockSpec((1, hs * PEER_TOPK, 128), lambda i, h: (i, h, 0))
    return pl.pallas_call(
        _peer_topk_kernel,
        grid=(nblk, PEER_HEADS // hs),
        in_specs=[pl.BlockSpec((128, hs * 2 * PEER_NKEYS), lambda i, h: (i, h)),
                  pl.BlockSpec((hs, 2, PEER_NKEYS, PEER_NKEYS), lambda i, h: (h, 0, 0, 0))],
        out_specs=[oblk, oblk],
        out_shape=[shape, jax.ShapeDtypeStruct(shape.shape, F32)],
        compiler_params=_params(("parallel", "parallel")),
        name="peer_topk",
    )(q, keys)


def _table_spec(table):
    return pl.BlockSpec(table.shape, lambda i: (0, 0), pipeline_mode=pl.Buffered(1))


def _gather_rows(idx_ref, tab_ref, rows_ref):
    n = 4 * PEER_PICKS

    def gather(t, _):
        base = pl.multiple_of(t * n, n)
        for k in range(PEER_PICKS):
            r = pl.multiple_of(idx_ref[t, k], 4)
            rows_ref[pl.ds(base + 4 * k, 4), :] = tab_ref[pl.ds(r, 4), :]
        return 0

    lax.fori_loop(0, PEER_TB, gather, 0)


def _token_rows(rows_ref, t):
    n = 4 * PEER_PICKS
    return pltpu.bitcast(rows_ref[pl.ds(pl.multiple_of(t * n, n), n), :], jnp.bfloat16)


def _own_row_mask():
    lane = lax.broadcasted_iota(jnp.int32, (8, 8 * PEER_PICKS), 1)
    sub = lax.broadcasted_iota(jnp.int32, (8, 8 * PEER_PICKS), 0)
    return (lane % 8) == (2 * (sub % 4) + sub // 4)


def _unpack(words):
    lo = lax.bitcast_convert_type(lax.shift_left(words, jnp.uint32(16)), F32)
    hi = lax.bitcast_convert_type(jnp.bitwise_and(words, jnp.uint32(0xFFFF0000)), F32)
    return lo, hi


def _peer_act_kernel(idx_ref, x_ref, gw_ref, tab_ref, o_ref, prod_ref):
    rows_per_tok = 4 * PEER_PICKS

    def gather(t, _):
        xlo = x_ref[t, 0:4, :]
        xhi = x_ref[t, 4:8, :]
        base = pl.multiple_of(t * rows_per_tok, rows_per_tok)
        for k in range(PEER_PICKS):
            r = pl.multiple_of(idx_ref[t, k], 4)
            lo, hi = _unpack(tab_ref[pl.ds(r, 4), :])
            prod_ref[pl.ds(base + 4 * k, 4), :] = lo * xlo + hi * xhi
        return 0

    lax.fori_loop(0, PEER_TB, gather, 0)

    ones = jnp.ones((128, 128), BF16)
    own = (lax.broadcasted_iota(jnp.int32, (rows_per_tok, PEER_PICKS), 0) // 4
           == lax.broadcasted_iota(jnp.int32, (rows_per_tok, PEER_PICKS), 1))
    own = jnp.where(own, 1.0, 0.0)

    def tail(g, _):
        acts = []
        for tt in range(8):
            base = pl.multiple_of((g * 8 + tt) * rows_per_tok, rows_per_tok)
            sums = _dot(prod_ref[pl.ds(base, rows_per_tok), :].astype(BF16), ones)
            acts.append(jnp.sum(sums * own, axis=0, keepdims=True))
        r0 = pl.multiple_of(g * 8, 8)
        o_ref[pl.ds(r0, 8), :] = _gelu(jnp.concatenate(acts, axis=0)) * gw_ref[pl.ds(r0, 8), :]
        return 0

    for g in range(PEER_TB // 8):
        tail(g, 0)


def _peer_act(idx, x8, gw, table):
    t = idx.shape[0]
    return pl.pallas_call(
        _peer_act_kernel,
        grid=(t // PEER_TB,),
        in_specs=[pl.BlockSpec((PEER_TB, PEER_PICKS), lambda i: (i, 0), memory_space=pltpu.SMEM),
                  pl.BlockSpec((PEER_TB, 8, 128), lambda i: (i, 0, 0)),
                  pl.BlockSpec((PEER_TB, PEER_PICKS), lambda i: (i, 0)),
                  _table_spec(table)],
        out_specs=pl.BlockSpec((PEER_TB, PEER_PICKS), lambda i: (i, 0)),
        out_shape=jax.ShapeDtypeStruct((t, PEER_PICKS), F32),
        scratch_shapes=[pltpu.VMEM((PEER_TB * 4 * PEER_PICKS, 128), F32)],
        compiler_params=_params(("arbitrary",)),
        name="peer_act",
    )(idx, x8, gw, table)


def _peer_out_kernel(idx_ref, w_ref, x_ref, tab_ref, o_ref, rows_ref, wrep_ref):
    _gather_rows(idx_ref, tab_ref, rows_ref)
    n = 8 * PEER_PICKS
    rep = (lax.broadcasted_iota(jnp.int32, (PEER_PICKS, n), 1) // 8
           == lax.broadcasted_iota(jnp.int32, (PEER_PICKS, n), 0))
    wrep_ref[...] = _dot(w_ref[...].astype(BF16), jnp.where(rep, 1.0, 0.0).astype(BF16))
    own = _own_row_mask()

    def tail(g, _):
        for tt in range(4):
            t = g * 4 + tt
            sel = jnp.where(own, jnp.broadcast_to(wrep_ref[pl.ds(t, 1), :], (8, n)), 0.0)
            o_ref[t] = x_ref[t] + _dot(sel.astype(jnp.bfloat16), _token_rows(rows_ref, t))
        return 0

    for g in range(PEER_TB // 4):
        tail(g, 0)


def _peer_out(idx, w, x8, table):
    t = idx.shape[0]
    return pl.pallas_call(
        _peer_out_kernel,
        grid=(t // PEER_TB,),
        in_specs=[pl.BlockSpec((PEER_TB, PEER_PICKS), lambda i: (i, 0), memory_space=pltpu.SMEM),
                  pl.BlockSpec((PEER_TB, PEER_PICKS), lambda i: (i, 0)),
                  pl.BlockSpec((PEER_TB, 8, 128), lambda i: (i, 0, 0)),
                  _table_spec(table)],
        out_specs=pl.BlockSpec((PEER_TB, 8, 128), lambda i: (i, 0, 0)),
        out_shape=jax.ShapeDtypeStruct((t, 8, 128), F32),
        scratch_shapes=[pltpu.VMEM((PEER_TB * 4 * PEER_PICKS, 128), jnp.uint32),
                        pltpu.VMEM((PEER_TB, 8 * PEER_PICKS), F32)],
        compiler_params=_params(("arbitrary",)),
        name="peer_out",
    )(idx, w, x8, table)


def _pack_table(tab):
    e, d = tab.shape
    bits = lax.bitcast_convert_type(tab.astype(jnp.bfloat16), jnp.uint16).astype(jnp.uint32)
    words = bits[:, :d // 2] | (bits[:, d // 2:] << 16)
    return words.reshape(e * 4, 128)


def _norm_kernel(x_ref, w_ref, o_ref):
    x = x_ref[...]
    o_ref[...] = x * lax.rsqrt(jnp.mean(x * x, axis=-1, keepdims=True) + NORM_EPS) * w_ref[...]


def _final_norm(x2, w):
    t, d = x2.shape
    tm = 512
    return pl.pallas_call(
        _norm_kernel,
        grid=(t // tm,),
        in_specs=[pl.BlockSpec((tm, d), lambda i: (i, 0)), pl.BlockSpec((1, d), lambda i: (0, 0))],
        out_specs=pl.BlockSpec((tm, d), lambda i: (i, 0)),
        out_shape=jax.ShapeDtypeStruct((t, d), F32),
        compiler_params=_params(("parallel",)),
        name="final_norm",
    )(x2, w)


def _permute_w_in(w):
    cols = [w[:, :2816],
            w[:, 2820:3076],
            w[:, 3076:3140], w[:, 3140:3204],
            w[:, 3204:3268], w[:, 3332:3396],
            w[:, 3268:3332], w[:, 3396:3460],
            w[:, 2816:2820], w[:, 3460:3472],
            jnp.zeros((w.shape[0], W_SMALL - 16), w.dtype)]
    return jnp.concatenate(cols, axis=1).astype(BF16)


def _pad_lanes(v, n=128):
    return jnp.pad(v, (0, n - v.shape[0]))[None, :]


def kernel(x, norm_mix, w_in, ret_norm, ssm_conv_w, ssm_conv_b, ssm_dt_bias, ssm_a_log, ssm_d, ssm_norm, nsa_pe_k, nsa_w1_k, nsa_w2_k, nsa_pe_v, nsa_w1_v, nsa_w2_v, w_out, norm_ffn, peer_wq, peer_keys, peer_u, peer_v, norm_final):
    b, s, d = x.shape
    t = b * s
    depth = w_in.shape[0]
    rc, ra, rb = (jnp.asarray(a) for a in _rope_tables())
    intra, qd, kd, cd = (jnp.asarray(a) for a in _ret_tables())
    ovlt = jnp.asarray(_nsa_overlap().T).astype(BF16)
    tril = jnp.asarray(np.tril(np.ones((SSM_CHUNK, SSM_CHUNK), np.float32))).astype(BF16)

    for i in range(depth):
        y_ret, y_moba, y_ssm, y_nsa, y_small = _inproj(
            x, norm_mix[i][None, :], _permute_w_in(w_in[i]), rc, ra, rb)
        o_ret = _retention(y_ret, intra, qd, kd, cd, ret_norm[i][None, :])
        o_moba = _moba(y_moba)
        o_ssm = _ssd(y_ssm, y_small, ssm_conv_w[i], ssm_conv_b[i][None, :],
                     _pad_lanes(jnp.pad(ssm_dt_bias[i], (0, 0))), _pad_lanes(ssm_a_log[i]),
                     jnp.repeat(ssm_d[i], HEAD_DIM)[None, :], ssm_norm[i][None, :], tril)
        k16 = y_nsa[:, :, 256:320].reshape(b, s // NSA_CMP_STRIDE, NSA_CMP_STRIDE * HEAD_DIM)
        v16 = y_nsa[:, :, 320:384].reshape(b, s // NSA_CMP_STRIDE, NSA_CMP_STRIDE * HEAD_DIM)
        kc, vc = _nsa_compress(k16, v16,
                               nsa_pe_k[i].reshape(1, -1), nsa_w1_k[i].astype(BF16), nsa_w2_k[i].astype(BF16),
                               nsa_pe_v[i].reshape(1, -1), nsa_w1_v[i].astype(BF16), nsa_w2_v[i].astype(BF16))
        o_nsa = _nsa_attend(y_nsa, y_small, kc, vc, ovlt)
        mixed = [o.reshape(t, GROUP_WIDTH) for o in (o_ret, o_moba, o_ssm, o_nsa)]
        x_mid, hn, q = _outproj(x.reshape(t, d), mixed, w_out[i].astype(BF16), norm_ffn[i][None, :],
                                peer_wq[i].astype(BF16))
        idx_t, gw_t = _peer_topk(q, peer_keys[i])
        idx = idx_t.transpose(0, 2, 1).reshape(t, PEER_PICKS)
        gw = gw_t.transpose(0, 2, 1).reshape(t, PEER_PICKS)
        wts = _peer_act(idx, hn.reshape(t, 8, 128), gw, _pack_table(peer_u[i]))
        x = _peer_out(idx, wts, x_mid.reshape(t, 8, 128), _pack_table(peer_v[i])).reshape(b, s, d)
    return _final_norm(x.reshape(t, d), norm_final[None, :]).reshape(b, s, d)
```

```python
import functools
import math

import numpy as np
import jax
import jax.numpy as jnp
from jax import lax
from jax.experimental import pallas as pl
from jax.experimental.pallas import tpu as pltpu

F32 = jnp.float32
BF16 = jnp.bfloat16

D_MODEL = 1024
SEQ = 2048
GROUP_WIDTH = 256
HEAD_DIM = 64
N_HEADS = 4
NORM_EPS = 1e-6
NEG = -1e30

ROPE_THETA = 500000.0
ROPE_DIMS = 16
RET_THETA = 10000.0
RET_CHUNK = 128
MOBA_BLOCK = 256
MOBA_TOPK = 3
SSM_CHUNK = 256
SSM_STATE = 128
SSM_CONV = 4
NSA_CMP_BLOCK = 32
NSA_CMP_STRIDE = 16
NSA_SLC_BLOCK = 64
NSA_SLC_TOPK = 16
NSA_WINDOW = 512
NSA_FORCE_BONUS = 1e6
PEER_HEADS = 8
PEER_NKEYS = 128
PEER_TOPK = 16
PEER_PICKS = PEER_HEADS * PEER_TOPK
PEER_TOPK_HEADS = 4
Q_BLOCK = 128

W_RET, W_MOBA, W_SSM, W_NSA, W_SMALL = 1024, 768, 1024, 640, 128
IN_COLS_PADDED = W_RET + W_MOBA + W_SSM + W_NSA + W_SMALL
ROPE_W = 512

IN_TM = 256
PEER_TB = 32
VMEM_LIMIT = 56 * 1024 * 1024


def _params(sem, vmem=VMEM_LIMIT):
    return pltpu.CompilerParams(dimension_semantics=sem, vmem_limit_bytes=vmem)


def _dot(a, b):
    return jnp.dot(a, b, preferred_element_type=F32)


def _dot_nt(a, b):
    return lax.dot_general(a, b, (((1,), (1,)), ((), ())), preferred_element_type=F32)


def _dot_tn(a, b):
    return lax.dot_general(a, b, (((0,), (0,)), ((), ())), preferred_element_type=F32)


def _split_bf16(x):
    hi = x.astype(BF16)
    lo = (x - hi.astype(F32)).astype(BF16)
    return hi, lo


def _silu(x):
    return x * (1.0 / (1.0 + jnp.exp(-x)))


def _gelu(x):
    return 0.5 * x * (1.0 + lax.erf(x * (1.0 / math.sqrt(2.0))))


def _rope_region(seq, theta, rot_dims, n_blocks, scale_blocks=(), ident_blocks=()):
    half = rot_dims // 2
    inv = 1.0 / (theta ** (np.arange(0, rot_dims, 2, dtype=np.float64) / rot_dims))
    ang = np.arange(seq, dtype=np.float64)[:, None] * inv[None, :]
    cos, sin = np.cos(ang), np.sin(ang)
    c = np.ones((seq, HEAD_DIM)); a = np.zeros((seq, HEAD_DIM)); b = np.zeros((seq, HEAD_DIM))
    c[:, :half] = cos; c[:, half:rot_dims] = cos
    a[:, :half] = -sin
    b[:, half:rot_dims] = sin
    cs, as_, bs = [], [], []
    for blk in range(n_blocks):
        if blk in ident_blocks:
            cs.append(np.ones_like(c)); as_.append(np.zeros_like(a)); bs.append(np.zeros_like(b))
            continue
        f = scale_blocks.get(blk, 1.0) if isinstance(scale_blocks, dict) else 1.0
        cs.append(c * f); as_.append(a * f); bs.append(b * f)
    return np.concatenate(cs, 1), np.concatenate(as_, 1), np.concatenate(bs, 1)


@functools.lru_cache(maxsize=None)
def _rope_tables():
    kscale = HEAD_DIM ** -0.5
    ret = _rope_region(SEQ, RET_THETA, HEAD_DIM, 8, scale_blocks={4: kscale, 5: kscale, 6: kscale, 7: kscale})
    moba = _rope_region(SEQ, ROPE_THETA, ROPE_DIMS, 8)
    nsa = _rope_region(SEQ, ROPE_THETA, ROPE_DIMS, 8, ident_blocks=(5,))
    return tuple(np.concatenate([ret[i], moba[i], nsa[i]], 1).astype(np.float32) for i in range(3))


@functools.lru_cache(maxsize=None)
def _ret_tables():
    h, c = N_HEADS, RET_CHUNK
    log_gamma = np.log1p(-np.exp2(-5.0 - np.arange(h, dtype=np.float64)))
    pos = np.arange(c, dtype=np.float64)
    diff = pos[:, None] - pos[None, :]
    intra = np.where(diff >= 0, np.exp(log_gamma[:, None, None] * np.maximum(diff, 0.0)), 0.0)
    q_dec = np.exp(log_gamma[:, None] * (pos + 1.0))
    k_dec = np.exp(log_gamma[:, None] * (c - 1.0 - pos))
    c_dec = np.exp(log_gamma * c)
    qd = np.repeat(q_dec.T, HEAD_DIM, axis=1)
    kd = np.repeat(k_dec.T, HEAD_DIM, axis=1)
    cd = np.broadcast_to(c_dec[:, None, None], (h, HEAD_DIM, HEAD_DIM))
    return (intra.astype(np.float32), qd.astype(np.float32), kd.astype(np.float32),
            np.ascontiguousarray(cd).astype(np.float32))


@functools.lru_cache(maxsize=None)
def _nsa_overlap():
    nc = (SEQ - NSA_CMP_BLOCK) // NSA_CMP_STRIDE + 1
    starts = np.arange(128) * NSA_CMP_STRIDE
    sb = np.arange(128)
    ov = ((starts[:, None] < (sb[None, :] + 1) * NSA_SLC_BLOCK)
          & ((starts + NSA_CMP_BLOCK)[:, None] > sb[None, :] * NSA_SLC_BLOCK))
    ov = ov & (np.arange(128)[:, None] < nc) & (sb[None, :] < SEQ // NSA_SLC_BLOCK)
    return ov.astype(np.float32)


def _inproj_kernel(x_ref, nw_ref, w_ref, rc_ref, ra_ref, rb_ref,
                   o_ret, o_moba, o_ssm, o_nsa, o_small):
    x = x_ref[0]
    ms = jnp.mean(x * x, axis=-1, keepdims=True)
    h = (x * lax.rsqrt(ms + NORM_EPS) * nw_ref[...]).astype(BF16)

    def rope(y, region, shift):
        sl = slice(region * ROPE_W, (region + 1) * ROPE_W)
        return (y * rc_ref[:, sl] + pltpu.roll(y, ROPE_W - shift, 1) * ra_ref[:, sl]
                + pltpu.roll(y, shift, 1) * rb_ref[:, sl])

    off = 0
    y = _dot(h, w_ref[:, off:off + W_RET]); off += W_RET
    o_ret[0, :, :ROPE_W] = rope(y[:, :ROPE_W], 0, HEAD_DIM // 2)
    o_ret[0, :, ROPE_W:] = y[:, ROPE_W:]
    y = _dot(h, w_ref[:, off:off + W_MOBA]); off += W_MOBA
    o_moba[0, :, :ROPE_W] = rope(y[:, :ROPE_W], 1, ROPE_DIMS // 2)
    o_moba[0, :, ROPE_W:] = y[:, ROPE_W:]
    o_ssm[0] = _dot(h, w_ref[:, off:off + W_SSM]); off += W_SSM
    y = _dot(h, w_ref[:, off:off + W_NSA]); off += W_NSA
    o_nsa[0, :, :ROPE_W] = rope(y[:, :ROPE_W], 2, ROPE_DIMS // 2)
    o_nsa[0, :, ROPE_W:] = y[:, ROPE_W:]
    o_small[0] = _dot(h, w_ref[:, off:off + W_SMALL])


def _inproj(x, nw, w, rc, ra, rb):
    b, s, d = x.shape
    ns = s // IN_TM
    widths = (W_RET, W_MOBA, W_SSM, W_NSA, W_SMALL)
    tab = pl.BlockSpec((IN_TM, 3 * ROPE_W), lambda si, bi: (si, 0))
    return pl.pallas_call(
        _inproj_kernel,
        grid=(ns, b),
        in_specs=[pl.BlockSpec((1, IN_TM, d), lambda si, bi: (bi, si, 0)),
                  pl.BlockSpec((1, d), lambda si, bi: (0, 0)),
                  pl.BlockSpec((d, IN_COLS_PADDED), lambda si, bi: (0, 0)),
                  tab, tab, tab],
        out_specs=[pl.BlockSpec((1, IN_TM, wd), lambda si, bi: (bi, si, 0)) for wd in widths],
        out_shape=[jax.ShapeDtypeStruct((b, s, wd), F32) for wd in widths],
        compiler_params=_params(("parallel", "parallel")),
        name="inproj",
    )(x, nw, w, rc, ra, rb)


def _ret_kernel(y_ref, intra_ref, qd_ref, kd_ref, cd_ref, nw_ref, o_ref, state_ref):
    @pl.when(pl.program_id(1) == 0)
    def _():
        state_ref[...] = jnp.zeros_like(state_ref)

    for h in range(N_HEADS):
        sl = slice(h * HEAD_DIM, (h + 1) * HEAD_DIM)
        q = y_ref[0, :, h * HEAD_DIM:(h + 1) * HEAD_DIM].astype(BF16)
        k = y_ref[0, :, GROUP_WIDTH + h * HEAD_DIM:GROUP_WIDTH + (h + 1) * HEAD_DIM]
        v = y_ref[0, :, 2 * GROUP_WIDTH + h * HEAD_DIM:2 * GROUP_WIDTH + (h + 1) * HEAD_DIM].astype(BF16)
        g = y_ref[0, :, 3 * GROUP_WIDTH + h * HEAD_DIM:3 * GROUP_WIDTH + (h + 1) * HEAD_DIM]
        att = _dot_nt(q, k.astype(BF16)) * intra_ref[h]
        o_in = _dot(att.astype(BF16), v)
        st = state_ref[h]
        o_x = _dot(q, st.astype(BF16)) * qd_ref[:, sl]
        kv = _dot_tn((k * kd_ref[:, sl]).astype(BF16), v)
        state_ref[h] = st * cd_ref[h] + kv
        o = o_in + o_x
        o = o * lax.rsqrt(jnp.mean(o * o, axis=-1, keepdims=True) + NORM_EPS) * nw_ref[:, sl]
        o_ref[0, :, sl] = _silu(g) * o


def _retention(y_ret, intra, qd, kd, cd, nw):
    b, s, _ = y_ret.shape
    nch = s // RET_CHUNK
    full = lambda shape: pl.BlockSpec(shape, lambda bi, ci: (0,) * len(shape))
    return pl.pallas_call(
        _ret_kernel,
        grid=(b, nch),
        in_specs=[pl.BlockSpec((1, RET_CHUNK, W_RET), lambda bi, ci: (bi, ci, 0)),
                  full(intra.shape), full(qd.shape), full(kd.shape), full(cd.shape), full(nw.shape)],
        out_specs=pl.BlockSpec((1, RET_CHUNK, GROUP_WIDTH), lambda bi, ci: (bi, ci, 0)),
        out_shape=jax.ShapeDtypeStruct((b, s, GROUP_WIDTH), F32),
        scratch_shapes=[pltpu.VMEM((N_HEADS, HEAD_DIM, HEAD_DIM), F32)],
        compiler_params=_params(("parallel", "arbitrary")),
        name="retention",
    )(y_ret, intra, qd, kd, cd, nw)


def _softmax_step(carry, s, ok, vb):
    m, l, acc = carry
    s = jnp.where(ok, s, NEG)
    m_new = jnp.maximum(m, jnp.max(s, axis=1, keepdims=True))
    p = jnp.exp(s - m_new)
    alpha = jnp.exp(m - m_new)
    l = alpha * l + jnp.sum(p, axis=1, keepdims=True)
    acc = alpha * acc + _dot(p.astype(BF16), vb)
    return m_new, l, acc


def _rank_rows(vals, n_cand):
    row = lax.broadcasted_iota(jnp.int32, vals.shape, 0)
    rank = jnp.zeros(vals.shape, F32)
    for m in range(n_cand):
        cand = vals[m:m + 1, :]
        beats = jnp.logical_or(cand > vals, jnp.logical_and(cand == vals, row > m))
        rank = rank + jnp.where(beats, 1.0, 0.0)
    return rank


MOBA_GATE_ROWS = 16


def _moba_kernel(q_ref, k_ref, v_ref, o_ref, km_ref):
    ci = pl.program_id(1)
    nb = SEQ // MOBA_BLOCK

    @pl.when(ci == 0)
    def _():
        km_ref[...] = jnp.zeros_like(km_ref)
        for n in range(nb):
            km_ref[n:n + 1, :] = jnp.mean(k_ref[0, n * MOBA_BLOCK:(n + 1) * MOBA_BLOCK, :], axis=0, keepdims=True)

    cur = ci // (MOBA_BLOCK // Q_BLOCK)
    t0 = ci * Q_BLOCK
    tpos = t0 + lax.broadcasted_iota(jnp.int32, (Q_BLOCK, 1), 0)
    blk = lax.broadcasted_iota(jnp.int32, (MOBA_GATE_ROWS, Q_BLOCK), 0)
    lane = lax.broadcasted_iota(jnp.int32, (Q_BLOCK, 128), 1)
    eye = jnp.where(lax.broadcasted_iota(jnp.int32, (MOBA_GATE_ROWS, 128), 0)
                    == lax.broadcasted_iota(jnp.int32, (MOBA_GATE_ROWS, 128), 1), 1.0, 0.0).astype(BF16)
    kcol = lax.broadcasted_iota(jnp.int32, (1, MOBA_BLOCK), 1)
    scale = HEAD_DIM ** -0.5

    heads = [slice(h * HEAD_DIM, (h + 1) * HEAD_DIM) for h in range(N_HEADS)]
    qs, allows = [], []
    for sl in heads:
        q = (q_ref[0, :, sl] * scale).astype(BF16)
        gate = _dot_nt(km_ref[:, sl].astype(BF16), q)
        gate = jnp.where(blk < cur, gate, NEG)
        rank = _rank_rows(gate, nb)
        allow = jnp.logical_or(jnp.logical_and(rank < MOBA_TOPK, blk < cur), blk == cur)
        qs.append(q)
        allows.append(_dot_tn(jnp.where(allow, 1.0, 0.0).astype(BF16), eye))

    def body(n, carry):
        start = pl.multiple_of(n * MOBA_BLOCK, MOBA_BLOCK)
        causal = (start + kcol) <= tpos
        out = []
        for sl, q, allow, (m, l, acc) in zip(heads, qs, allows, carry):
            kb = k_ref[0, pl.ds(start, MOBA_BLOCK), sl].astype(BF16)
            vb = v_ref[0, pl.ds(start, MOBA_BLOCK), sl].astype(BF16)
            acol = jnp.sum(jnp.where(lane == n, allow, 0.0), axis=1, keepdims=True)
            ok = jnp.logical_and(acol > 0.5, causal)
            out.append(_softmax_step((m, l, acc), _dot_nt(q, kb), ok, vb))
        return tuple(out)

    init = (jnp.full((Q_BLOCK, 1), NEG, F32), jnp.zeros((Q_BLOCK, 1), F32),
            jnp.zeros((Q_BLOCK, HEAD_DIM), F32))
    final = lax.fori_loop(0, cur + 1, body, (init,) * N_HEADS)
    for sl, (m, l, acc) in zip(heads, final):
        o_ref[0, :, sl] = acc / l


def _moba(y_moba):
    b, s, _ = y_moba.shape
    nq = s // Q_BLOCK
    return pl.pallas_call(
        _moba_kernel,
        grid=(b, nq),
        in_specs=[pl.BlockSpec((1, Q_BLOCK, GROUP_WIDTH), lambda bi, ci: (bi, ci, 0)),
                  pl.BlockSpec((1, s, GROUP_WIDTH), lambda bi, ci: (bi, 0, 1)),
                  pl.BlockSpec((1, s, GROUP_WIDTH), lambda bi, ci: (bi, 0, 2))],
        out_specs=pl.BlockSpec((1, Q_BLOCK, GROUP_WIDTH), lambda bi, ci: (bi, ci, 0)),
        out_shape=jax.ShapeDtypeStruct((b, s, GROUP_WIDTH), F32),
        scratch_shapes=[pltpu.VMEM((MOBA_GATE_ROWS, GROUP_WIDTH), F32)],
        compiler_params=_params(("parallel", "arbitrary")),
        name="moba",
    )(y_moba, y_moba, y_moba)


def _ssd_kernel(z_ref, x_ref, b_ref, c_ref, dt_ref, cw_ref, cb_ref, dtb_ref, alog_ref,
                dskip_ref, nw_ref, tril_ref, o_ref, buf_ref, state_ref):
    L = SSM_CHUNK
    ci = pl.program_id(1)

    @pl.when(ci == 0)
    def _():
        buf_ref[...] = jnp.zeros_like(buf_ref)
        state_ref[...] = jnp.zeros_like(state_ref)

    convd = []
    for j, ref in enumerate((x_ref, b_ref, c_ref)):
        cs = slice(j * GROUP_WIDTH, (j + 1) * GROUP_WIDTH)
        buf_ref[8:8 + L, cs] = ref[0]
        acc = cb_ref[:, cs] + cw_ref[SSM_CONV - 1:SSM_CONV, cs] * ref[0]
        for i in range(SSM_CONV - 1):
            shift = SSM_CONV - 1 - i
            acc = acc + cw_ref[i:i + 1, cs] * buf_ref[8 - shift:8 - shift + L, cs]
        convd.append(_silu(acc))
    buf_ref[0:8, :] = buf_ref[L:L + 8, :]
    xs, bm, cm = convd

    dtr = dt_ref[0] + dtb_ref[...]
    dt = jnp.maximum(dtr, 0.0) + jnp.log1p(jnp.exp(-jnp.abs(dtr)))
    a = -jnp.exp(alog_ref[...])
    da = dt * a
    d1, d2, d3 = da.astype(BF16), None, None
    r1 = da - d1.astype(F32)
    d2 = r1.astype(BF16)
    d3 = (r1 - d2.astype(F32)).astype(BF16)
    tril = tril_ref[...]
    acol = _dot(tril, d1) + _dot(tril, d2) + _dot(tril, d3)
    arow = acol.T
    alast = acol[L - 1:L, :]
    row = lax.broadcasted_iota(jnp.int32, (L, L), 0)
    col = lax.broadcasted_iota(jnp.int32, (L, L), 1)
    causal = row >= col

    gmats = []
    for g in range(2):
        gs = slice(g * SSM_STATE, (g + 1) * SSM_STATE)
        gmats.append(_dot_nt(cm[:, gs].astype(BF16), bm[:, gs].astype(BF16)))

    for h in range(N_HEADS):
        sl = slice(h * HEAD_DIM, (h + 1) * HEAD_DIM)
        g = h // 2
        gs = slice(g * SSM_STATE, (g + 1) * SSM_STATE)
        a_c = acol[:, h:h + 1]
        a_r = arow[h:h + 1, :]
        a_l = alast[:, h:h + 1]
        decay = jnp.where(causal, jnp.exp(jnp.where(causal, a_c - a_r, 0.0)), 0.0)
        xh = xs[:, sl]
        xdt = xh * dt[:, h:h + 1]
        y_diag = _dot((gmats[g] * decay).astype(BF16), xdt.astype(BF16))
        to_end = jnp.exp(a_l - a_c)
        st_new = _dot_tn((xdt * to_end).astype(BF16), bm[:, gs].astype(BF16))
        prev = state_ref[h]
        y_off = _dot_nt(cm[:, gs].astype(BF16), prev.astype(BF16)) * jnp.exp(a_c)
        state_ref[h] = prev * jnp.exp(a_l) + st_new
        o_ref[0, :, sl] = y_diag + y_off + xh * dskip_ref[:, sl]

    y = o_ref[0] * _silu(z_ref[0])
    for g in range(2):
        gs = slice(g * 128, (g + 1) * 128)
        yg = y[:, gs]
        o_ref[0, :, gs] = yg * lax.rsqrt(jnp.mean(yg * yg, axis=-1, keepdims=True) + NORM_EPS) * nw_ref[:, gs]


def _ssd(y_ssm, y_small, cw, cb, dtb, alog, dskip, nw, tril):
    b, s, _ = y_ssm.shape
    nch = s // SSM_CHUNK
    L = SSM_CHUNK
    full = lambda shape: pl.BlockSpec(shape, lambda bi, ci: (0,) * len(shape))
    colblk = lambda j: pl.BlockSpec((1, L, GROUP_WIDTH), lambda bi, ci, j=j: (bi, ci, j))
    return pl.pallas_call(
        _ssd_kernel,
        grid=(b, nch),
        in_specs=[colblk(0), colblk(1), colblk(2), colblk(3),
                  pl.BlockSpec((1, L, W_SMALL), lambda bi, ci: (bi, ci, 0)),
                  full(cw.shape), full(cb.shape), full(dtb.shape), full(alog.shape),
                  full(dskip.shape), full(nw.shape), full(tril.shape)],
        out_specs=pl.BlockSpec((1, L, GROUP_WIDTH), lambda bi, ci: (bi, ci, 0)),
        out_shape=jax.ShapeDtypeStruct((b, s, GROUP_WIDTH), F32),
        scratch_shapes=[pltpu.VMEM((L + 8, 3 * GROUP_WIDTH), F32),
                        pltpu.VMEM((N_HEADS, HEAD_DIM, SSM_STATE), F32)],
        compiler_params=_params(("parallel", "arbitrary")),
        name="ssd",
    )(y_ssm, y_ssm, y_ssm, y_ssm, y_small, cw, cb, dtb, alog, dskip, nw, tril)


def _nsa_cmp_kernel(k_ref, v_ref, pek_ref, w1k_ref, w2k_ref, pev_ref, w1v_ref, w2v_ref, kc_ref, vc_ref):
    half = NSA_CMP_STRIDE * HEAD_DIM
    for x_ref, pe_ref, w1_ref, w2_ref, o_ref in ((k_ref, pek_ref, w1k_ref, w2k_ref, kc_ref),
                                                 (v_ref, pev_ref, w1v_ref, w2v_ref, vc_ref)):
        x = x_ref[0]
        a = _dot((x + pe_ref[:, :half]).astype(BF16), w1_ref[:half, :])
        bb = _dot((x + pe_ref[:, half:]).astype(BF16), w1_ref[half:, :])
        h1 = a + pltpu.roll(bb, bb.shape[0] - 1, 0)
        o_ref[0] = _dot(_gelu(h1).astype(BF16), w2_ref[...])


def _nsa_compress(k16, v16, pek, w1k, w2k, pev, w1v, w2v):
    b = k16.shape[0]
    full = lambda shape: pl.BlockSpec(shape, lambda bi: (0,) * len(shape))
    blk = pl.BlockSpec((1, 128, NSA_CMP_STRIDE * HEAD_DIM), lambda bi: (bi, 0, 0))
    oblk = pl.BlockSpec((1, 128, HEAD_DIM), lambda bi: (bi, 0, 0))
    return pl.pallas_call(
        _nsa_cmp_kernel,
        grid=(b,),
        in_specs=[blk, blk, full(pek.shape), full(w1k.shape), full(w2k.shape),
                  full(pev.shape), full(w1v.shape), full(w2v.shape)],
        out_specs=[oblk, oblk],
        out_shape=[jax.ShapeDtypeStruct((b, 128, HEAD_DIM), F32)] * 2,
        compiler_params=_params(("parallel",)),
        name="nsa_compress",
    )(k16, v16, pek, w1k, w2k, pev, w1v, w2v)


def _nsa_kernel(q_ref, kk_ref, vv_ref, kc_ref, vc_ref, g_ref, ovlt_ref, o_ref):
    ci = pl.program_id(1)
    t0 = ci * Q_BLOCK
    scale = HEAD_DIM ** -0.5
    R = N_HEADS * Q_BLOCK
    q4 = jnp.concatenate([q_ref[0, :, h * HEAD_DIM:(h + 1) * HEAD_DIM] for h in range(N_HEADS)], axis=0)
    q4 = (q4 * scale).astype(BF16)
    tq = t0 + lax.broadcasted_iota(jnp.int32, (Q_BLOCK, 1), 0)
    tq4 = jnp.concatenate([tq] * N_HEADS, axis=0)
    lane4 = lax.broadcasted_iota(jnp.int32, (R, 128), 1)

    s = _dot_nt(q4, kc_ref[0].astype(BF16))
    ok = (lane4 * NSA_CMP_STRIDE + (NSA_CMP_BLOCK - 1)) <= tq4
    s = jnp.where(ok, s, NEG)
    e = jnp.where(ok, jnp.exp(s - jnp.max(s, axis=1, keepdims=True)), 0.0)
    l = jnp.sum(e, axis=1, keepdims=True)
    p_cmp = e / jnp.where(l > 0.0, l, 1.0)
    pb = p_cmp.astype(BF16)
    o_cmp = _dot(pb, vc_ref[0].astype(BF16))

    nsb = SEQ // NSA_SLC_BLOCK
    imp = sum(_dot_nt(ovlt_ref[...], pb[h * Q_BLOCK:(h + 1) * Q_BLOCK]) for h in range(N_HEADS))[0:nsb]
    sblk = lax.broadcasted_iota(jnp.int32, (nsb, Q_BLOCK), 0)
    cur_b = (t0 + lax.broadcasted_iota(jnp.int32, (1, Q_BLOCK), 1)) // NSA_SLC_BLOCK
    forced = jnp.logical_or(jnp.logical_or(sblk == 0, sblk == cur_b), sblk == cur_b - 1)
    imp = jnp.where(forced, imp + NSA_FORCE_BONUS, imp)
    imp = jnp.where(sblk <= cur_b, imp, NEG)
    sel = jnp.where(_rank_rows(imp, nsb) < NSA_SLC_TOPK, 1.0, 0.0).astype(BF16)

    init = (jnp.full((R, 1), NEG, F32), jnp.zeros((R, 1), F32), jnp.zeros((R, HEAD_DIM), F32))

    TS = 256
    jrow = lax.broadcasted_iota(jnp.int32, (nsb, TS), 0)
    kcol_s = lax.broadcasted_iota(jnp.int32, (1, TS), 1)
    jcol = lax.broadcasted_iota(jnp.int32, (nsb, TS), 1) // NSA_SLC_BLOCK

    def slc_body(n, carry):
        start = pl.multiple_of(n * TS, TS)
        kb = kk_ref[0, pl.ds(start, TS), 0:HEAD_DIM].astype(BF16)
        vb = vv_ref[0, pl.ds(start, TS), 0:HEAD_DIM].astype(BF16)
        expand = jnp.where(jrow == n * (TS // NSA_SLC_BLOCK) + jcol, 1.0, 0.0).astype(BF16)
        selk = _dot_tn(sel, expand)
        okq = jnp.logical_and(selk > 0.5, (start + kcol_s) <= tq)
        okf = jnp.where(okq, 1.0, 0.0)
        ok4 = jnp.concatenate([okf] * N_HEADS, axis=0) > 0.5
        return _softmax_step(carry, _dot_nt(q4, kb), ok4, vb)

    _, l_s, acc_s = lax.fori_loop(0, ci // (TS // Q_BLOCK) + 1, slc_body, init)
    o_slc = acc_s / l_s

    TW = Q_BLOCK
    kcol_w = lax.broadcasted_iota(jnp.int32, (1, TW), 1)
    n_win = NSA_WINDOW // TW + 1
    scores, values = [], []
    for w in range(n_win):
        n = ci - (n_win - 1) + w
        start = pl.multiple_of(jnp.maximum(n, 0) * TW, TW)
        kb = kk_ref[0, pl.ds(start, TW), HEAD_DIM:2 * HEAD_DIM].astype(BF16)
        values.append(vv_ref[0, pl.ds(start, TW), HEAD_DIM:2 * HEAD_DIM].astype(BF16))
        kpos = start + kcol_w
        ok = jnp.logical_and(jnp.logical_and(kpos <= tq4, kpos > tq4 - NSA_WINDOW), n >= 0)
        scores.append(jnp.where(ok, _dot_nt(q4, kb), NEG))
    m_w = functools.reduce(jnp.maximum, [jnp.max(sc, axis=1, keepdims=True) for sc in scores])
    probs = [jnp.exp(sc - m_w) for sc in scores]
    l_w = sum(jnp.sum(p, axis=1, keepdims=True) for p in probs)
    o_win = sum(_dot(p.astype(BF16), vb) for p, vb in zip(probs, values)) / l_w

    gates = 1.0 / (1.0 + jnp.exp(-g_ref[0]))
    for h in range(N_HEADS):
        rs = slice(h * Q_BLOCK, (h + 1) * Q_BLOCK)
        o = (gates[:, 4 + h:5 + h] * o_cmp[rs] + gates[:, 8 + h:9 + h] * o_slc[rs]
             + gates[:, 12 + h:13 + h] * o_win[rs])
        o_ref[0, :, h * HEAD_DIM:(h + 1) * HEAD_DIM] = o


def _nsa_attend(y_nsa, y_small, kc, vc, ovlt):
    b, s, _ = y_nsa.shape
    nq = s // Q_BLOCK
    return pl.pallas_call(
        _nsa_kernel,
        grid=(b, nq),
        in_specs=[pl.BlockSpec((1, Q_BLOCK, GROUP_WIDTH), lambda bi, ci: (bi, ci, 0)),
                  pl.BlockSpec((1, s, 128), lambda bi, ci: (bi, 0, 3)),
                  pl.BlockSpec((1, s, 128), lambda bi, ci: (bi, 0, 4)),
                  pl.BlockSpec((1, 128, HEAD_DIM), lambda bi, ci: (bi, 0, 0)),
                  pl.BlockSpec((1, 128, HEAD_DIM), lambda bi, ci: (bi, 0, 0)),
                  pl.BlockSpec((1, Q_BLOCK, W_SMALL), lambda bi, ci: (bi, ci, 0)),
                  pl.BlockSpec((128, 128), lambda bi, ci: (0, 0))],
        out_specs=pl.BlockSpec((1, Q_BLOCK, GROUP_WIDTH), lambda bi, ci: (bi, ci, 0)),
        out_shape=jax.ShapeDtypeStruct((b, s, GROUP_WIDTH), F32),
        compiler_params=_params(("parallel", "parallel")),
        name="nsa_attend",
    )(y_nsa, y_nsa, y_nsa, kc, vc, y_small, ovlt)


def _outproj_kernel(x_ref, a_ref, b_ref, c_ref, d_ref, wo_ref, nw_ref, wq_ref, xo_ref, hn_ref, q_ref):
    acc = x_ref[...]
    for j, ref in enumerate((a_ref, b_ref, c_ref, d_ref)):
        acc = acc + _dot(ref[...].astype(BF16), wo_ref[j * GROUP_WIDTH:(j + 1) * GROUP_WIDTH, :])
    xo_ref[...] = acc
    hn = acc * lax.rsqrt(jnp.mean(acc * acc, axis=-1, keepdims=True) + NORM_EPS) * nw_ref[...]
    hn_ref[...] = hn
    q_ref[...] = _dot(hn.astype(BF16), wq_ref[...])


def _outproj(x2, mixed, wo, nw, wq):
    t, d = x2.shape
    nq = wq.shape[1]
    row = lambda wd: pl.BlockSpec((IN_TM, wd), lambda i: (i, 0))
    full = lambda shape: pl.BlockSpec(shape, lambda i: (0,) * len(shape))
    return pl.pallas_call(
        _outproj_kernel,
        grid=(t // IN_TM,),
        in_specs=[row(d)] + [row(GROUP_WIDTH)] * 4 + [full(wo.shape), full(nw.shape), full(wq.shape)],
        out_specs=[row(d), row(d), row(nq)],
        out_shape=[jax.ShapeDtypeStruct((t, d), F32), jax.ShapeDtypeStruct((t, d), F32),
                   jax.ShapeDtypeStruct((t, nq), F32)],
        compiler_params=_params(("parallel",)),
        name="outproj",
    )(x2, *mixed, wo, nw, wq)


def _topk_rows(v, k, payload=None):
    rows = lax.broadcasted_iota(jnp.int32, v.shape, 0).astype(F32)
    big = float(v.shape[0])
    vals, idxs = [], []
    for _ in range(k):
        m = jnp.max(v, axis=0, keepdims=True)
        first = jnp.min(jnp.where(v == m, rows, big), axis=0, keepdims=True)
        hit = rows == first
        vals.append(m)
        if payload is None:
            idxs.append(first)
        else:
            idxs.append(jnp.max(jnp.where(hit, payload, -1.0), axis=0, keepdims=True))
        v = jnp.where(hit, -jnp.inf, v)
    return jnp.concatenate(vals, axis=0), jnp.concatenate(idxs, axis=0)


_PAIR_RANGES = ((0, 24), (24, 29), (32, 36), (40, 43), (44, 46), (48, 50), (52, 54), (56, 64))


def _pruned_pairs(p1, p2, comb):
    lo = lax.broadcasted_iota(jnp.int32, (8, 128), 0) < 4
    p2a = p2[0:8]
    p2d = jnp.where(lo, p2a, pltpu.roll(p2a, 4, 0))
    blocks = [comb(p1[0:1], p2a), comb(p1[0:1], p2[8:16]), comb(p1[1:2], p2a), comb(p1[2:3], p2a),
              comb(p1[3:4], p2a), comb(jnp.where(lo, p1[4:5], p1[5:6]), p2d),
              comb(jnp.where(lo, p1[6:7], p1[7:8]), p2d), comb(p1[8:16], p2[0:1])]
    return jnp.concatenate(blocks, axis=0)


def _peer_topk_kernel(q_ref, keys_ref, idx_ref, gw_ref):
    for h in range(PEER_TOPK_HEADS):
        q = q_ref[:, 2 * PEER_NKEYS * h:2 * PEER_NKEYS * (h + 1)].astype(BF16)
        s1 = _dot_nt(keys_ref[h, 0].astype(BF16), q[:, :PEER_NKEYS])
        s2 = _dot_nt(keys_ref[h, 1].astype(BF16), q[:, PEER_NKEYS:])
        v1, i1 = _topk_rows(s1, PEER_TOPK)
        v2, i2 = _topk_rows(s2, PEER_TOPK)
        cand = _pruned_pairs(v1, v2, lambda a, b: a + b)
        cidx = _pruned_pairs(i1, i2, lambda a, b: a * float(PEER_NKEYS) + b)
        row = lax.broadcasted_iota(jnp.int32, cand.shape, 0)
        valid = functools.reduce(jnp.logical_or, [jnp.logical_and(row >= lo, row < hi) for lo, hi in _PAIR_RANGES])
        top_s, top_i = _topk_rows(jnp.where(valid, cand, -jnp.inf), PEER_TOPK, payload=cidx)
        e = jnp.exp(top_s - top_s[0:1, :])
        rows = slice(h * PEER_TOPK, (h + 1) * PEER_TOPK)
        gw_ref[0, rows, :] = e / jnp.sum(e, axis=0, keepdims=True)
        idx_ref[0, rows, :] = (top_i * 4.0).astype(jnp.int32)


def _peer_topk(q, keys):
    t = q.shape[0]
    nblk = t // 128
    hs = PEER_TOPK_HEADS
    shape = jax.ShapeDtypeStruct((nblk, PEER_PICKS, 128), jnp.int32)
    oblk = pl.BlockSpec((1, hs * PEER_TOPK, 128), lambda i, h: (i, h, 0))
    return pl.pallas_call(
        _peer_topk_kernel,
        grid=(nblk, PEER_HEADS // hs),
        in_specs=[pl.BlockSpec((128, hs * 2 * PEER_NKEYS), lambda i, h: (i, h)),
                  pl.BlockSpec((hs, 2, PEER_NKEYS, PEER_NKEYS), lambda i, h: (h, 0, 0, 0))],
        out_specs=[oblk, oblk],
        out_shape=[shape, jax.ShapeDtypeStruct(shape.shape, F32)],
        compiler_params=_params(("parallel", "parallel")),
        name="peer_topk",
    )(q, keys)


def _table_spec(table):
    return pl.BlockSpec(table.shape, lambda i: (0, 0), pipeline_mode=pl.Buffered(1))


PEER_ROWS = 4 * PEER_PICKS
PEER_HALF = PEER_TB * PEER_ROWS
PEER_UNROLL = 4


def _pipe_step(buf_a, buf_b, step):
    i = pl.program_id(0)

    @pl.when(i == 0)
    def _():
        buf_b[...] = jnp.zeros(buf_b.shape, buf_b.dtype)

    @pl.when(i % 2 == 0)
    def _():
        step(buf_a, buf_b)

    @pl.when(i % 2 == 1)
    def _():
        step(buf_b, buf_a)


def _cur_block(nblk):
    return lambda i: (jnp.minimum(i, nblk - 1), 0)


def _prev_block(ndim):
    return lambda i: (jnp.maximum(i - 1, 0),) + (0,) * (ndim - 1)


def _own_row_mask():
    lane = lax.broadcasted_iota(jnp.int32, (8, 8 * PEER_PICKS), 1)
    sub = lax.broadcasted_iota(jnp.int32, (8, 8 * PEER_PICKS), 0)
    return (lane % 8) == (2 * (sub % 4) + sub // 4)


def _unpack(words):
    lo = lax.bitcast_convert_type(lax.shift_left(words, jnp.uint32(16)), F32)
    hi = lax.bitcast_convert_type(jnp.bitwise_and(words, jnp.uint32(0xFFFF0000)), F32)
    return lo, hi


def _peer_act_kernel(idx_ref, x_ref, gw_ref, tab_ref, o_ref, prod_a, prod_b, act_ref):
    ones = jnp.ones((128, 128), BF16)
    own = (lax.broadcasted_iota(jnp.int32, (PEER_ROWS, PEER_PICKS), 0) // 4
           == lax.broadcasted_iota(jnp.int32, (PEER_ROWS, PEER_PICKS), 1))
    own = jnp.where(own, 1.0, 0.0)

    def step(cur_ref, prev_ref):
        def token(t, _):
            off = pl.multiple_of(t * PEER_ROWS, PEER_ROWS)
            sums = _dot(prev_ref[pl.ds(off, PEER_ROWS), :].astype(BF16), ones)
            act_ref[pl.ds(t, 1), :] = jnp.sum(sums * own, axis=0, keepdims=True)
            xlo = x_ref[t, 0:4, :]
            xhi = x_ref[t, 4:8, :]
            for k in range(PEER_PICKS):
                r = pl.multiple_of(idx_ref[t, k], 4)
                lo, hi = _unpack(tab_ref[pl.ds(r, 4), :])
                cur_ref[pl.ds(off + 4 * k, 4), :] = lo * xlo + hi * xhi
            return 0

        def group(g, _):
            for tt in range(PEER_UNROLL):
                token(g * PEER_UNROLL + tt, 0)
            return 0

        lax.fori_loop(0, PEER_TB // PEER_UNROLL, group, 0)

    _pipe_step(prod_a, prod_b, step)
    o_ref[...] = _gelu(act_ref[...]) * gw_ref[...]


def _peer_act(idx, x8, gw, table):
    t = idx.shape[0]
    nblk = t // PEER_TB
    return pl.pallas_call(
        _peer_act_kernel,
        grid=(nblk + 1,),
        in_specs=[pl.BlockSpec((PEER_TB, PEER_PICKS), _cur_block(nblk), memory_space=pltpu.SMEM),
                  pl.BlockSpec((PEER_TB, 8, 128), lambda i: (jnp.minimum(i, nblk - 1), 0, 0)),
                  pl.BlockSpec((PEER_TB, PEER_PICKS), _prev_block(2)),
                  _table_spec(table)],
        out_specs=pl.BlockSpec((PEER_TB, PEER_PICKS), _prev_block(2)),
        out_shape=jax.ShapeDtypeStruct((t, PEER_PICKS), F32),
        scratch_shapes=[pltpu.VMEM((PEER_HALF, 128), F32), pltpu.VMEM((PEER_HALF, 128), F32),
                        pltpu.VMEM((PEER_TB, PEER_PICKS), F32)],
        compiler_params=_params(("arbitrary",)),
        name="peer_act",
    )(idx, x8, gw, table)


def _gather_rows(idx_ref, tab_ref, rows_ref):
    n = 4 * PEER_PICKS

    def gather(t, _):
        base = pl.multiple_of(t * n, n)
        for k in range(PEER_PICKS):
            r = pl.multiple_of(idx_ref[t, k], 4)
            rows_ref[pl.ds(base + 4 * k, 4), :] = tab_ref[pl.ds(r, 4), :]
        return 0

    lax.fori_loop(0, PEER_TB, gather, 0)


def _token_rows(rows_ref, t):
    n = 4 * PEER_PICKS
    return pltpu.bitcast(rows_ref[pl.ds(pl.multiple_of(t * n, n), n), :], jnp.bfloat16)


def _peer_out_kernel(idx_ref, w_ref, x_ref, tab_ref, o_ref, rows_ref, wrep_ref):
    _gather_rows(idx_ref, tab_ref, rows_ref)
    n = 8 * PEER_PICKS
    rep = (lax.broadcasted_iota(jnp.int32, (PEER_PICKS, n), 1) // 8
           == lax.broadcasted_iota(jnp.int32, (PEER_PICKS, n), 0))
    wrep_ref[...] = _dot(w_ref[...].astype(BF16), jnp.where(rep, 1.0, 0.0).astype(BF16))
    own = _own_row_mask()

    def tail(g, _):
        for tt in range(4):
            t = g * 4 + tt
            sel = jnp.where(own, jnp.broadcast_to(wrep_ref[pl.ds(t, 1), :], (8, n)), 0.0)
            o_ref[t] = x_ref[t] + _dot(sel.astype(jnp.bfloat16), _token_rows(rows_ref, t))
        return 0

    for g in range(PEER_TB // 4):
        tail(g, 0)


def _peer_out(idx, w, x8, table):
    t = idx.shape[0]
    return pl.pallas_call(
        _peer_out_kernel,
        grid=(t // PEER_TB,),
        in_specs=[pl.BlockSpec((PEER_TB, PEER_PICKS), lambda i: (i, 0), memory_space=pltpu.SMEM),
                  pl.BlockSpec((PEER_TB, PEER_PICKS), lambda i: (i, 0)),
                  pl.BlockSpec((PEER_TB, 8, 128), lambda i: (i, 0, 0)),
                  _table_spec(table)],
        out_specs=pl.BlockSpec((PEER_TB, 8, 128), lambda i: (i, 0, 0)),
        out_shape=jax.ShapeDtypeStruct((t, 8, 128), F32),
        scratch_shapes=[pltpu.VMEM((PEER_TB * 4 * PEER_PICKS, 128), jnp.uint32),
                        pltpu.VMEM((PEER_TB, 8 * PEER_PICKS), F32)],
        compiler_params=_params(("arbitrary",)),
        name="peer_out",
    )(idx, w, x8, table)


def _pack_table(tab):
    e, d = tab.shape
    bits = lax.bitcast_convert_type(tab.astype(jnp.bfloat16), jnp.uint16).astype(jnp.uint32)
    words = bits[:, :d // 2] | (bits[:, d // 2:] << 16)
    return words.reshape(e * 4, 128)


def _norm_kernel(x_ref, w_ref, o_ref):
    x = x_ref[...]
    o_ref[...] = x * lax.rsqrt(jnp.mean(x * x, axis=-1, keepdims=True) + NORM_EPS) * w_ref[...]


def _final_norm(x2, w):
    t, d = x2.shape
    tm = 512
    return pl.pallas_call(
        _norm_kernel,
        grid=(t // tm,),
        in_specs=[pl.BlockSpec((tm, d), lambda i: (i, 0)), pl.BlockSpec((1, d), lambda i: (0, 0))],
        out_specs=pl.BlockSpec((tm, d), lambda i: (i, 0)),
        out_shape=jax.ShapeDtypeStruct((t, d), F32),
        compiler_params=_params(("parallel",)),
        name="final_norm",
    )(x2, w)


def _permute_w_in(w):
    cols = [w[:, :2816],
            w[:, 2820:3076],
            w[:, 3076:3140], w[:, 3140:3204],
            w[:, 3204:3268], w[:, 3332:3396],
            w[:, 3268:3332], w[:, 3396:3460],
            w[:, 2816:2820], w[:, 3460:3472],
            jnp.zeros((w.shape[0], W_SMALL - 16), w.dtype)]
    return jnp.concatenate(cols, axis=1).astype(BF16)


def _pad_lanes(v, n=128):
    return jnp.pad(v, (0, n - v.shape[0]))[None, :]


def kernel(x, norm_mix, w_in, ret_norm, ssm_conv_w, ssm_conv_b, ssm_dt_bias, ssm_a_log, ssm_d, ssm_norm, nsa_pe_k, nsa_w1_k, nsa_w2_k, nsa_pe_v, nsa_w1_v, nsa_w2_v, w_out, norm_ffn, peer_wq, peer_keys, peer_u, peer_v, norm_final):
    b, s, d = x.shape
    t = b * s
    depth = w_in.shape[0]
    rc, ra, rb = (jnp.asarray(a) for a in _rope_tables())
    intra, qd, kd, cd = (jnp.asarray(a) for a in _ret_tables())
    ovlt = jnp.asarray(_nsa_overlap().T).astype(BF16)
    tril = jnp.asarray(np.tril(np.ones((SSM_CHUNK, SSM_CHUNK), np.float32))).astype(BF16)

    for i in range(depth):
        y_ret, y_moba, y_ssm, y_nsa, y_small = _inproj(
            x, norm_mix[i][None, :], _permute_w_in(w_in[i]), rc, ra, rb)
        o_ret = _retention(y_ret, intra, qd, kd, cd, ret_norm[i][None, :])
        o_moba = _moba(y_moba)
        o_ssm = _ssd(y_ssm, y_small, ssm_conv_w[i], ssm_conv_b[i][None, :],
                     _pad_lanes(jnp.pad(ssm_dt_bias[i], (0, 0))), _pad_lanes(ssm_a_log[i]),
                     jnp.repeat(ssm_d[i], HEAD_DIM)[None, :], ssm_norm[i][None, :], tril)
        k16 = y_nsa[:, :, 256:320].reshape(b, s // NSA_CMP_STRIDE, NSA_CMP_STRIDE * HEAD_DIM)
        v16 = y_nsa[:, :, 320:384].reshape(b, s // NSA_CMP_STRIDE, NSA_CMP_STRIDE * HEAD_DIM)
        kc, vc = _nsa_compress(k16, v16,
                               nsa_pe_k[i].reshape(1, -1), nsa_w1_k[i].astype(BF16), nsa_w2_k[i].astype(BF16),
                               nsa_pe_v[i].reshape(1, -1), nsa_w1_v[i].astype(BF16), nsa_w2_v[i].astype(BF16))
        o_nsa = _nsa_attend(y_nsa, y_small, kc, vc, ovlt)
        mixed = [o.reshape(t, GROUP_WIDTH) for o in (o_ret, o_moba, o_ssm, o_nsa)]
        x_mid, hn, q = _outproj(x.reshape(t, d), mixed, w_out[i].astype(BF16), norm_ffn[i][None, :],
                                peer_wq[i].astype(BF16))
        idx_t, gw_t = _peer_topk(q, peer_keys[i])
        idx = idx_t.transpose(0, 2, 1).reshape(t, PEER_PICKS)
        gw = gw_t.transpose(0, 2, 1).reshape(t, PEER_PICKS)
        wts = _peer_act(idx, hn.reshape(t, 8, 128), gw, _pack_table(peer_u[i]))
        x = _peer_out(idx, wts, x_mid.reshape(t, 8, 128), _pack_table(peer_v[i])).reshape(b, s, d)
    return _final_norm(x.reshape(t, d), norm_final[None, :]).reshape(b, s, d)
```

```python
import functools
import math

import numpy as np
import jax
import jax.numpy as jnp
from jax import lax
from jax.experimental import pallas as pl
from jax.experimental.pallas import tpu as pltpu

F32 = jnp.float32
BF16 = jnp.bfloat16

D_MODEL = 1024
SEQ = 2048
GROUP_WIDTH = 256
HEAD_DIM = 64
N_HEADS = 4
NORM_EPS = 1e-6
NEG = -1e30

ROPE_THETA = 500000.0
ROPE_DIMS = 16
RET_THETA = 10000.0
RET_CHUNK = 128
MOBA_BLOCK = 256
MOBA_TOPK = 3
SSM_CHUNK = 256
SSM_STATE = 128
SSM_CONV = 4
NSA_CMP_BLOCK = 32
NSA_CMP_STRIDE = 16
NSA_SLC_BLOCK = 64
NSA_SLC_TOPK = 16
NSA_WINDOW = 512
NSA_FORCE_BONUS = 1e6
PEER_HEADS = 8
PEER_NKEYS = 128
PEER_TOPK = 16
PEER_PICKS = PEER_HEADS * PEER_TOPK
PEER_TOPK_HEADS = 4
Q_BLOCK = 128

W_RET, W_MOBA, W_SSM, W_NSA, W_SMALL = 1024, 768, 1024, 640, 128
IN_COLS_PADDED = W_RET + W_MOBA + W_SSM + W_NSA + W_SMALL
ROPE_W = 512

IN_TM = 256
PEER_TB = 32
VMEM_LIMIT = 56 * 1024 * 1024


def _params(sem, vmem=VMEM_LIMIT):
    return pltpu.CompilerParams(dimension_semantics=sem, vmem_limit_bytes=vmem)


def _dot(a, b):
    return jnp.dot(a, b, preferred_element_type=F32)


def _dot_nt(a, b):
    return lax.dot_general(a, b, (((1,), (1,)), ((), ())), preferred_element_type=F32)


def _dot_tn(a, b):
    return lax.dot_general(a, b, (((0,), (0,)), ((), ())), preferred_element_type=F32)


def _split_bf16(x):
    hi = x.astype(BF16)
    lo = (x - hi.astype(F32)).astype(BF16)
    return hi, lo


def _silu(x):
    return x * (1.0 / (1.0 + jnp.exp(-x)))


def _gelu(x):
    return 0.5 * x * (1.0 + lax.erf(x * (1.0 / math.sqrt(2.0))))


def _rope_region(seq, theta, rot_dims, n_blocks, scale_blocks=(), ident_blocks=()):
    half = rot_dims // 2
    inv = 1.0 / (theta ** (np.arange(0, rot_dims, 2, dtype=np.float64) / rot_dims))
    ang = np.arange(seq, dtype=np.float64)[:, None] * inv[None, :]
    cos, sin = np.cos(ang), np.sin(ang)
    c = np.ones((seq, HEAD_DIM)); a = np.zeros((seq, HEAD_DIM)); b = np.zeros((seq, HEAD_DIM))
    c[:, :half] = cos; c[:, half:rot_dims] = cos
    a[:, :half] = -sin
    b[:, half:rot_dims] = sin
    cs, as_, bs = [], [], []
    for blk in range(n_blocks):
        if blk in ident_blocks:
            cs.append(np.ones_like(c)); as_.append(np.zeros_like(a)); bs.append(np.zeros_like(b))
            continue
        f = scale_blocks.get(blk, 1.0) if isinstance(scale_blocks, dict) else 1.0
        cs.append(c * f); as_.append(a * f); bs.append(b * f)
    return np.concatenate(cs, 1), np.concatenate(as_, 1), np.concatenate(bs, 1)


@functools.lru_cache(maxsize=None)
def _rope_tables():
    kscale = HEAD_DIM ** -0.5
    ret = _rope_region(SEQ, RET_THETA, HEAD_DIM, 8, scale_blocks={4: kscale, 5: kscale, 6: kscale, 7: kscale})
    moba = _rope_region(SEQ, ROPE_THETA, ROPE_DIMS, 8)
    nsa = _rope_region(SEQ, ROPE_THETA, ROPE_DIMS, 8, ident_blocks=(5,))
    return tuple(np.concatenate([ret[i], moba[i], nsa[i]], 1).astype(np.float32) for i in range(3))


@functools.lru_cache(maxsize=None)
def _ret_tables():
    h, c = N_HEADS, RET_CHUNK
    log_gamma = np.log1p(-np.exp2(-5.0 - np.arange(h, dtype=np.float64)))
    pos = np.arange(c, dtype=np.float64)
    diff = pos[:, None] - pos[None, :]
    intra = np.where(diff >= 0, np.exp(log_gamma[:, None, None] * np.maximum(diff, 0.0)), 0.0)
    q_dec = np.exp(log_gamma[:, None] * (pos + 1.0))
    k_dec = np.exp(log_gamma[:, None] * (c - 1.0 - pos))
    c_dec = np.exp(log_gamma * c)
    qd = np.repeat(q_dec.T, HEAD_DIM, axis=1)
    kd = np.repeat(k_dec.T, HEAD_DIM, axis=1)
    cd = np.broadcast_to(c_dec[:, None, None], (h, HEAD_DIM, HEAD_DIM))
    return (intra.astype(np.float32), qd.astype(np.float32), kd.astype(np.float32),
            np.ascontiguousarray(cd).astype(np.float32))


@functools.lru_cache(maxsize=None)
def _nsa_overlap():
    nc = (SEQ - NSA_CMP_BLOCK) // NSA_CMP_STRIDE + 1
    starts = np.arange(128) * NSA_CMP_STRIDE
    sb = np.arange(128)
    ov = ((starts[:, None] < (sb[None, :] + 1) * NSA_SLC_BLOCK)
          & ((starts + NSA_CMP_BLOCK)[:, None] > sb[None, :] * NSA_SLC_BLOCK))
    ov = ov & (np.arange(128)[:, None] < nc) & (sb[None, :] < SEQ // NSA_SLC_BLOCK)
    return ov.astype(np.float32)


def _inproj_kernel(x_ref, nw_ref, w_ref, rc_ref, ra_ref, rb_ref,
                   o_ret, o_moba, o_ssm, o_nsa, o_small):
    x = x_ref[0]
    ms = jnp.mean(x * x, axis=-1, keepdims=True)
    h = (x * lax.rsqrt(ms + NORM_EPS) * nw_ref[...]).astype(BF16)

    def rope(y, region, shift):
        sl = slice(region * ROPE_W, (region + 1) * ROPE_W)
        return (y * rc_ref[:, sl] + pltpu.roll(y, ROPE_W - shift, 1) * ra_ref[:, sl]
                + pltpu.roll(y, shift, 1) * rb_ref[:, sl])

    off = 0
    y = _dot(h, w_ref[:, off:off + W_RET]); off += W_RET
    o_ret[0, :, :ROPE_W] = rope(y[:, :ROPE_W], 0, HEAD_DIM // 2)
    o_ret[0, :, ROPE_W:] = y[:, ROPE_W:]
    y = _dot(h, w_ref[:, off:off + W_MOBA]); off += W_MOBA
    o_moba[0, :, :ROPE_W] = rope(y[:, :ROPE_W], 1, ROPE_DIMS // 2)
    o_moba[0, :, ROPE_W:] = y[:, ROPE_W:]
    o_ssm[0] = _dot(h, w_ref[:, off:off + W_SSM]); off += W_SSM
    y = _dot(h, w_ref[:, off:off + W_NSA]); off += W_NSA
    o_nsa[0, :, :ROPE_W] = rope(y[:, :ROPE_W], 2, ROPE_DIMS // 2)
    o_nsa[0, :, ROPE_W:] = y[:, ROPE_W:]
    o_small[0] = _dot(h, w_ref[:, off:off + W_SMALL])


def _inproj(x, nw, w, rc, ra, rb):
    b, s, d = x.shape
    ns = s // IN_TM
    widths = (W_RET, W_MOBA, W_SSM, W_NSA, W_SMALL)
    tab = pl.BlockSpec((IN_TM, 3 * ROPE_W), lambda si, bi: (si, 0))
    return pl.pallas_call(
        _inproj_kernel,
        grid=(ns, b),
        in_specs=[pl.BlockSpec((1, IN_TM, d), lambda si, bi: (bi, si, 0)),
                  pl.BlockSpec((1, d), lambda si, bi: (0, 0)),
                  pl.BlockSpec((d, IN_COLS_PADDED), lambda si, bi: (0, 0)),
                  tab, tab, tab],
        out_specs=[pl.BlockSpec((1, IN_TM, wd), lambda si, bi: (bi, si, 0)) for wd in widths],
        out_shape=[jax.ShapeDtypeStruct((b, s, wd), F32) for wd in widths],
        compiler_params=_params(("parallel", "parallel")),
        name="inproj",
    )(x, nw, w, rc, ra, rb)


def _ret_kernel(y_ref, intra_ref, qd_ref, kd_ref, cd_ref, nw_ref, o_ref, state_ref):
    @pl.when(pl.program_id(1) == 0)
    def _():
        state_ref[...] = jnp.zeros_like(state_ref)

    for h in range(N_HEADS):
        sl = slice(h * HEAD_DIM, (h + 1) * HEAD_DIM)
        q = y_ref[0, :, h * HEAD_DIM:(h + 1) * HEAD_DIM].astype(BF16)
        k = y_ref[0, :, GROUP_WIDTH + h * HEAD_DIM:GROUP_WIDTH + (h + 1) * HEAD_DIM]
        v = y_ref[0, :, 2 * GROUP_WIDTH + h * HEAD_DIM:2 * GROUP_WIDTH + (h + 1) * HEAD_DIM].astype(BF16)
        g = y_ref[0, :, 3 * GROUP_WIDTH + h * HEAD_DIM:3 * GROUP_WIDTH + (h + 1) * HEAD_DIM]
        att = _dot_nt(q, k.astype(BF16)) * intra_ref[h]
        o_in = _dot(att.astype(BF16), v)
        st = state_ref[h]
        o_x = _dot(q, st.astype(BF16)) * qd_ref[:, sl]
        kv = _dot_tn((k * kd_ref[:, sl]).astype(BF16), v)
        state_ref[h] = st * cd_ref[h] + kv
        o = o_in + o_x
        o = o * lax.rsqrt(jnp.mean(o * o, axis=-1, keepdims=True) + NORM_EPS) * nw_ref[:, sl]
        o_ref[0, :, sl] = _silu(g) * o


def _retention(y_ret, intra, qd, kd, cd, nw):
    b, s, _ = y_ret.shape
    nch = s // RET_CHUNK
    full = lambda shape: pl.BlockSpec(shape, lambda bi, ci: (0,) * len(shape))
    return pl.pallas_call(
        _ret_kernel,
        grid=(b, nch),
        in_specs=[pl.BlockSpec((1, RET_CHUNK, W_RET), lambda bi, ci: (bi, ci, 0)),
                  full(intra.shape), full(qd.shape), full(kd.shape), full(cd.shape), full(nw.shape)],
        out_specs=pl.BlockSpec((1, RET_CHUNK, GROUP_WIDTH), lambda bi, ci: (bi, ci, 0)),
        out_shape=jax.ShapeDtypeStruct((b, s, GROUP_WIDTH), F32),
        scratch_shapes=[pltpu.VMEM((N_HEADS, HEAD_DIM, HEAD_DIM), F32)],
        compiler_params=_params(("parallel", "arbitrary")),
        name="retention",
    )(y_ret, intra, qd, kd, cd, nw)


def _softmax_step(carry, s, ok, vb1):
    m, acc = carry
    s = jnp.where(ok, s, NEG)
    m_new = jnp.maximum(m, jnp.max(s, axis=1, keepdims=True))
    p = jnp.exp(s - m_new)
    acc = jnp.exp(m - m_new) * acc + _dot(p.astype(BF16), vb1)
    return m_new, acc


def _values_and_ones(pair, upper):
    lane = lax.broadcasted_iota(jnp.int32, pair.shape, 1)
    keep = (lane >= HEAD_DIM) if upper else (lane < HEAD_DIM)
    return jnp.where(keep, pair, 1.0).astype(BF16)


def _normalized(acc, upper):
    if upper:
        return acc[:, HEAD_DIM:] / acc[:, 0:1]
    return acc[:, :HEAD_DIM] / acc[:, HEAD_DIM:HEAD_DIM + 1]


def _rank_rows(vals, n_cand):
    row = lax.broadcasted_iota(jnp.int32, vals.shape, 0)
    rank = jnp.zeros(vals.shape, F32)
    for m in range(n_cand):
        cand = vals[m:m + 1, :]
        beats = jnp.logical_or(cand > vals, jnp.logical_and(cand == vals, row > m))
        rank = rank + jnp.where(beats, 1.0, 0.0)
    return rank


MOBA_GATE_ROWS = 16


def _moba_kernel(q_ref, k_ref, v_ref, o_ref, km_ref):
    ci = pl.program_id(1)
    nb = SEQ // MOBA_BLOCK

    @pl.when(ci == 0)
    def _():
        km_ref[...] = jnp.zeros_like(km_ref)
        for n in range(nb):
            km_ref[n:n + 1, :] = jnp.mean(k_ref[0, n * MOBA_BLOCK:(n + 1) * MOBA_BLOCK, :], axis=0, keepdims=True)

    cur = ci // (MOBA_BLOCK // Q_BLOCK)
    t0 = ci * Q_BLOCK
    tpos = t0 + lax.broadcasted_iota(jnp.int32, (Q_BLOCK, 1), 0)
    blk = lax.broadcasted_iota(jnp.int32, (MOBA_GATE_ROWS, Q_BLOCK), 0)
    lane = lax.broadcasted_iota(jnp.int32, (Q_BLOCK, 128), 1)
    eye = jnp.where(lax.broadcasted_iota(jnp.int32, (MOBA_GATE_ROWS, 128), 0)
                    == lax.broadcasted_iota(jnp.int32, (MOBA_GATE_ROWS, 128), 1), 1.0, 0.0).astype(BF16)
    kcol = lax.broadcasted_iota(jnp.int32, (1, MOBA_BLOCK), 1)
    scale = HEAD_DIM ** -0.5

    heads = [slice(h * HEAD_DIM, (h + 1) * HEAD_DIM) for h in range(N_HEADS)]
    qs, allows = [], []
    for sl in heads:
        q = (q_ref[0, :, sl] * scale).astype(BF16)
        gate = _dot_nt(km_ref[:, sl].astype(BF16), q)
        gate = jnp.where(blk < cur, gate, NEG)
        rank = _rank_rows(gate, nb)
        allow = jnp.logical_or(jnp.logical_and(rank < MOBA_TOPK, blk < cur), blk == cur)
        qs.append(q)
        allows.append(_dot_tn(jnp.where(allow, 1.0, 0.0).astype(BF16), eye))

    def body(n, carry):
        start = pl.multiple_of(n * MOBA_BLOCK, MOBA_BLOCK)
        causal = (start + kcol) <= tpos
        out = []
        for h, (sl, q, allow, (m, acc)) in enumerate(zip(heads, qs, allows, carry)):
            kb = k_ref[0, pl.ds(start, MOBA_BLOCK), sl].astype(BF16)
            pair = v_ref[0, pl.ds(start, MOBA_BLOCK), (h // 2) * 128:(h // 2 + 1) * 128]
            acol = jnp.sum(jnp.where(lane == n, allow, 0.0), axis=1, keepdims=True)
            ok = jnp.logical_and(acol > 0.5, causal)
            out.append(_softmax_step((m, acc), _dot_nt(q, kb), ok, _values_and_ones(pair, h % 2 == 1)))
        return tuple(out)

    init = (jnp.full((Q_BLOCK, 1), NEG, F32), jnp.zeros((Q_BLOCK, 128), F32))
    final = lax.fori_loop(0, cur + 1, body, (init,) * N_HEADS)
    for h, (sl, (m, acc)) in enumerate(zip(heads, final)):
        o_ref[0, :, sl] = _normalized(acc, h % 2 == 1)


def _moba(y_moba):
    b, s, _ = y_moba.shape
    nq = s // Q_BLOCK
    return pl.pallas_call(
        _moba_kernel,
        grid=(b, nq),
        in_specs=[pl.BlockSpec((1, Q_BLOCK, GROUP_WIDTH), lambda bi, ci: (bi, ci, 0)),
                  pl.BlockSpec((1, s, GROUP_WIDTH), lambda bi, ci: (bi, 0, 1)),
                  pl.BlockSpec((1, s, GROUP_WIDTH), lambda bi, ci: (bi, 0, 2))],
        out_specs=pl.BlockSpec((1, Q_BLOCK, GROUP_WIDTH), lambda bi, ci: (bi, ci, 0)),
        out_shape=jax.ShapeDtypeStruct((b, s, GROUP_WIDTH), F32),
        scratch_shapes=[pltpu.VMEM((MOBA_GATE_ROWS, GROUP_WIDTH), F32)],
        compiler_params=_params(("parallel", "arbitrary")),
        name="moba",
    )(y_moba, y_moba, y_moba)


def _ssd_kernel(z_ref, x_ref, b_ref, c_ref, dt_ref, cw_ref, cb_ref, dtb_ref, alog_ref,
                dskip_ref, nw_ref, tril_ref, o_ref, buf_ref, state_ref):
    L = SSM_CHUNK
    ci = pl.program_id(1)

    @pl.when(ci == 0)
    def _():
        buf_ref[...] = jnp.zeros_like(buf_ref)
        state_ref[...] = jnp.zeros_like(state_ref)

    convd = []
    for j, ref in enumerate((x_ref, b_ref, c_ref)):
        cs = slice(j * GROUP_WIDTH, (j + 1) * GROUP_WIDTH)
        buf_ref[8:8 + L, cs] = ref[0]
        acc = cb_ref[:, cs] + cw_ref[SSM_CONV - 1:SSM_CONV, cs] * ref[0]
        for i in range(SSM_CONV - 1):
            shift = SSM_CONV - 1 - i
            acc = acc + cw_ref[i:i + 1, cs] * buf_ref[8 - shift:8 - shift + L, cs]
        convd.append(_silu(acc))
    buf_ref[0:8, :] = buf_ref[L:L + 8, :]
    xs, bm, cm = convd

    dtr = dt_ref[0] + dtb_ref[...]
    dt = jnp.maximum(dtr, 0.0) + jnp.log1p(jnp.exp(-jnp.abs(dtr)))
    a = -jnp.exp(alog_ref[...])
    da = dt * a
    d1, d2, d3 = da.astype(BF16), None, None
    r1 = da - d1.astype(F32)
    d2 = r1.astype(BF16)
    d3 = (r1 - d2.astype(F32)).astype(BF16)
    tril = tril_ref[...]
    acol = _dot(tril, d1) + _dot(tril, d2) + _dot(tril, d3)
    arow = acol.T
    alast = acol[L - 1:L, :]
    row = lax.broadcasted_iota(jnp.int32, (L, L), 0)
    col = lax.broadcasted_iota(jnp.int32, (L, L), 1)
    causal = row >= col

    gmats = []
    for g in range(2):
        gs = slice(g * SSM_STATE, (g + 1) * SSM_STATE)
        gmats.append(_dot_nt(cm[:, gs].astype(BF16), bm[:, gs].astype(BF16)))

    for h in range(N_HEADS):
        sl = slice(h * HEAD_DIM, (h + 1) * HEAD_DIM)
        g = h // 2
        gs = slice(g * SSM_STATE, (g + 1) * SSM_STATE)
        a_c = acol[:, h:h + 1]
        a_r = arow[h:h + 1, :]
        a_l = alast[:, h:h + 1]
        decay = jnp.where(causal, jnp.exp(jnp.where(causal, a_c - a_r, 0.0)), 0.0)
        xh = xs[:, sl]
        xdt = xh * dt[:, h:h + 1]
        y_diag = _dot((gmats[g] * decay).astype(BF16), xdt.astype(BF16))
        to_end = jnp.exp(a_l - a_c)
        st_new = _dot_tn((xdt * to_end).astype(BF16), bm[:, gs].astype(BF16))
        prev = state_ref[h]
        y_off = _dot_nt(cm[:, gs].astype(BF16), prev.astype(BF16)) * jnp.exp(a_c)
        state_ref[h] = prev * jnp.exp(a_l) + st_new
        o_ref[0, :, sl] = y_diag + y_off + xh * dskip_ref[:, sl]

    y = o_ref[0] * _silu(z_ref[0])
    for g in range(2):
        gs = slice(g * 128, (g + 1) * 128)
        yg = y[:, gs]
        o_ref[0, :, gs] = yg * lax.rsqrt(jnp.mean(yg * yg, axis=-1, keepdims=True) + NORM_EPS) * nw_ref[:, gs]


def _ssd(y_ssm, y_small, cw, cb, dtb, alog, dskip, nw, tril):
    b, s, _ = y_ssm.shape
    nch = s // SSM_CHUNK
    L = SSM_CHUNK
    full = lambda shape: pl.BlockSpec(shape, lambda bi, ci: (0,) * len(shape))
    colblk = lambda j: pl.BlockSpec((1, L, GROUP_WIDTH), lambda bi, ci, j=j: (bi, ci, j))
    return pl.pallas_call(
        _ssd_kernel,
        grid=(b, nch),
        in_specs=[colblk(0), colblk(1), colblk(2), colblk(3),
                  pl.BlockSpec((1, L, W_SMALL), lambda bi, ci: (bi, ci, 0)),
                  full(cw.shape), full(cb.shape), full(dtb.shape), full(alog.shape),
                  full(dskip.shape), full(nw.shape), full(tril.shape)],
        out_specs=pl.BlockSpec((1, L, GROUP_WIDTH), lambda bi, ci: (bi, ci, 0)),
        out_shape=jax.ShapeDtypeStruct((b, s, GROUP_WIDTH), F32),
        scratch_shapes=[pltpu.VMEM((L + 8, 3 * GROUP_WIDTH), F32),
                        pltpu.VMEM((N_HEADS, HEAD_DIM, SSM_STATE), F32)],
        compiler_params=_params(("parallel", "arbitrary")),
        name="ssd",
    )(y_ssm, y_ssm, y_ssm, y_ssm, y_small, cw, cb, dtb, alog, dskip, nw, tril)


def _nsa_cmp_kernel(k_ref, v_ref, pek_ref, w1k_ref, w2k_ref, pev_ref, w1v_ref, w2v_ref, kc_ref, vc_ref):
    half = NSA_CMP_STRIDE * HEAD_DIM
    for x_ref, pe_ref, w1_ref, w2_ref, o_ref in ((k_ref, pek_ref, w1k_ref, w2k_ref, kc_ref),
                                                 (v_ref, pev_ref, w1v_ref, w2v_ref, vc_ref)):
        x = x_ref[0]
        a = _dot((x + pe_ref[:, :half]).astype(BF16), w1_ref[:half, :])
        bb = _dot((x + pe_ref[:, half:]).astype(BF16), w1_ref[half:, :])
        h1 = a + pltpu.roll(bb, bb.shape[0] - 1, 0)
        o_ref[0] = _dot(_gelu(h1).astype(BF16), w2_ref[...])


def _nsa_compress(k16, v16, pek, w1k, w2k, pev, w1v, w2v):
    b = k16.shape[0]
    full = lambda shape: pl.BlockSpec(shape, lambda bi: (0,) * len(shape))
    blk = pl.BlockSpec((1, 128, NSA_CMP_STRIDE * HEAD_DIM), lambda bi: (bi, 0, 0))
    oblk = pl.BlockSpec((1, 128, HEAD_DIM), lambda bi: (bi, 0, 0))
    return pl.pallas_call(
        _nsa_cmp_kernel,
        grid=(b,),
        in_specs=[blk, blk, full(pek.shape), full(w1k.shape), full(w2k.shape),
                  full(pev.shape), full(w1v.shape), full(w2v.shape)],
        out_specs=[oblk, oblk],
        out_shape=[jax.ShapeDtypeStruct((b, 128, HEAD_DIM), F32)] * 2,
        compiler_params=_params(("parallel",)),
        name="nsa_compress",
    )(k16, v16, pek, w1k, w2k, pev, w1v, w2v)


def _nsa_kernel(q_ref, kk_ref, vv_ref, kc_ref, vc_ref, g_ref, ovlt_ref, o_ref):
    ci = pl.program_id(1)
    t0 = ci * Q_BLOCK
    scale = HEAD_DIM ** -0.5
    R = N_HEADS * Q_BLOCK
    q4 = jnp.concatenate([q_ref[0, :, h * HEAD_DIM:(h + 1) * HEAD_DIM] for h in range(N_HEADS)], axis=0)
    q4 = (q4 * scale).astype(BF16)
    tq = t0 + lax.broadcasted_iota(jnp.int32, (Q_BLOCK, 1), 0)
    tq4 = jnp.concatenate([tq] * N_HEADS, axis=0)
    lane4 = lax.broadcasted_iota(jnp.int32, (R, 128), 1)

    s = _dot_nt(q4, kc_ref[0].astype(BF16))
    ok = (lane4 * NSA_CMP_STRIDE + (NSA_CMP_BLOCK - 1)) <= tq4
    s = jnp.where(ok, s, NEG)
    e = jnp.where(ok, jnp.exp(s - jnp.max(s, axis=1, keepdims=True)), 0.0)
    l = jnp.sum(e, axis=1, keepdims=True)
    p_cmp = e / jnp.where(l > 0.0, l, 1.0)
    pb = p_cmp.astype(BF16)
    o_cmp = _dot(pb, vc_ref[0].astype(BF16))

    nsb = SEQ // NSA_SLC_BLOCK
    imp = sum(_dot_nt(ovlt_ref[...], pb[h * Q_BLOCK:(h + 1) * Q_BLOCK]) for h in range(N_HEADS))[0:nsb]
    sblk = lax.broadcasted_iota(jnp.int32, (nsb, Q_BLOCK), 0)
    cur_b = (t0 + lax.broadcasted_iota(jnp.int32, (1, Q_BLOCK), 1)) // NSA_SLC_BLOCK
    forced = jnp.logical_or(jnp.logical_or(sblk == 0, sblk == cur_b), sblk == cur_b - 1)
    imp = jnp.where(forced, imp + NSA_FORCE_BONUS, imp)
    imp = jnp.where(sblk <= cur_b, imp, NEG)
    sel = jnp.where(_rank_rows(imp, nsb) < NSA_SLC_TOPK, 1.0, 0.0).astype(BF16)

    init = (jnp.full((R, 1), NEG, F32), jnp.zeros((R, 128), F32))

    TS = 256
    jrow = lax.broadcasted_iota(jnp.int32, (nsb, TS), 0)
    kcol_s = lax.broadcasted_iota(jnp.int32, (1, TS), 1)
    jcol = lax.broadcasted_iota(jnp.int32, (nsb, TS), 1) // NSA_SLC_BLOCK

    def slc_body(n, carry):
        start = pl.multiple_of(n * TS, TS)
        kb = kk_ref[0, pl.ds(start, TS), 0:HEAD_DIM].astype(BF16)
        vb1 = _values_and_ones(vv_ref[0, pl.ds(start, TS), :], False)
        expand = jnp.where(jrow == n * (TS // NSA_SLC_BLOCK) + jcol, 1.0, 0.0).astype(BF16)
        selk = _dot_tn(sel, expand)
        okq = jnp.logical_and(selk > 0.5, (start + kcol_s) <= tq)
        okf = jnp.where(okq, 1.0, 0.0)
        ok4 = jnp.concatenate([okf] * N_HEADS, axis=0) > 0.5
        return _softmax_step(carry, _dot_nt(q4, kb), ok4, vb1)

    _, acc_s = lax.fori_loop(0, ci // (TS // Q_BLOCK) + 1, slc_body, init)
    o_slc = _normalized(acc_s, False)

    TW = Q_BLOCK
    kcol_w = lax.broadcasted_iota(jnp.int32, (1, TW), 1)
    n_win = NSA_WINDOW // TW + 1
    scores, values = [], []
    for w in range(n_win):
        n = ci - (n_win - 1) + w
        start = pl.multiple_of(jnp.maximum(n, 0) * TW, TW)
        kb = kk_ref[0, pl.ds(start, TW), HEAD_DIM:2 * HEAD_DIM].astype(BF16)
        values.append(_values_and_ones(vv_ref[0, pl.ds(start, TW), :], True))
        kpos = start + kcol_w
        ok = jnp.logical_and(jnp.logical_and(kpos <= tq4, kpos > tq4 - NSA_WINDOW), n >= 0)
        scores.append(jnp.where(ok, _dot_nt(q4, kb), NEG))
    m_w = functools.reduce(jnp.maximum, [jnp.max(sc, axis=1, keepdims=True) for sc in scores])
    probs = [jnp.exp(sc - m_w) for sc in scores]
    o_win = _normalized(sum(_dot(p.astype(BF16), vb1) for p, vb1 in zip(probs, values)), True)

    gates = 1.0 / (1.0 + jnp.exp(-g_ref[0]))
    for h in range(N_HEADS):
        rs = slice(h * Q_BLOCK, (h + 1) * Q_BLOCK)
        o = (gates[:, 4 + h:5 + h] * o_cmp[rs] + gates[:, 8 + h:9 + h] * o_slc[rs]
             + gates[:, 12 + h:13 + h] * o_win[rs])
        o_ref[0, :, h * HEAD_DIM:(h + 1) * HEAD_DIM] = o


def _nsa_attend(y_nsa, y_small, kc, vc, ovlt):
    b, s, _ = y_nsa.shape
    nq = s // Q_BLOCK
    return pl.pallas_call(
        _nsa_kernel,
        grid=(b, nq),
        in_specs=[pl.BlockSpec((1, Q_BLOCK, GROUP_WIDTH), lambda bi, ci: (bi, ci, 0)),
                  pl.BlockSpec((1, s, 128), lambda bi, ci: (bi, 0, 3)),
                  pl.BlockSpec((1, s, 128), lambda bi, ci: (bi, 0, 4)),
                  pl.BlockSpec((1, 128, HEAD_DIM), lambda bi, ci: (bi, 0, 0)),
                  pl.BlockSpec((1, 128, HEAD_DIM), lambda bi, ci: (bi, 0, 0)),
                  pl.BlockSpec((1, Q_BLOCK, W_SMALL), lambda bi, ci: (bi, ci, 0)),
                  pl.BlockSpec((128, 128), lambda bi, ci: (0, 0))],
        out_specs=pl.BlockSpec((1, Q_BLOCK, GROUP_WIDTH), lambda bi, ci: (bi, ci, 0)),
        out_shape=jax.ShapeDtypeStruct((b, s, GROUP_WIDTH), F32),
        compiler_params=_params(("parallel", "parallel")),
        name="nsa_attend",
    )(y_nsa, y_nsa, y_nsa, kc, vc, y_small, ovlt)


def _outproj_kernel(x_ref, a_ref, b_ref, c_ref, d_ref, wo_ref, nw_ref, wq_ref, xo_ref, hn_ref, q_ref):
    acc = x_ref[...]
    for j, ref in enumerate((a_ref, b_ref, c_ref, d_ref)):
        acc = acc + _dot(ref[...].astype(BF16), wo_ref[j * GROUP_WIDTH:(j + 1) * GROUP_WIDTH, :])
    xo_ref[...] = acc
    hn = acc * lax.rsqrt(jnp.mean(acc * acc, axis=-1, keepdims=True) + NORM_EPS) * nw_ref[...]
    hn_ref[...] = hn
    q_ref[...] = _dot(hn.astype(BF16), wq_ref[...])


def _outproj(x2, mixed, wo, nw, wq):
    t, d = x2.shape
    nq = wq.shape[1]
    row = lambda wd: pl.BlockSpec((IN_TM, wd), lambda i: (i, 0))
    full = lambda shape: pl.BlockSpec(shape, lambda i: (0,) * len(shape))
    return pl.pallas_call(
        _outproj_kernel,
        grid=(t // IN_TM,),
        in_specs=[row(d)] + [row(GROUP_WIDTH)] * 4 + [full(wo.shape), full(nw.shape), full(wq.shape)],
        out_specs=[row(d), row(d), row(nq)],
        out_shape=[jax.ShapeDtypeStruct((t, d), F32), jax.ShapeDtypeStruct((t, d), F32),
                   jax.ShapeDtypeStruct((t, nq), F32)],
        compiler_params=_params(("parallel",)),
        name="outproj",
    )(x2, *mixed, wo, nw, wq)


def _topk_rows(v, k, payload=None):
    rows = lax.broadcasted_iota(jnp.int32, v.shape, 0).astype(F32)
    big = float(v.shape[0])
    vals, idxs = [], []
    for _ in range(k):
        m = jnp.max(v, axis=0, keepdims=True)
        first = jnp.min(jnp.where(v == m, rows, big), axis=0, keepdims=True)
        hit = rows == first
        vals.append(m)
        if payload is None:
            idxs.append(first)
        else:
            idxs.append(jnp.max(jnp.where(hit, payload, -1.0), axis=0, keepdims=True))
        v = jnp.where(hit, -jnp.inf, v)
    return jnp.concatenate(vals, axis=0), jnp.concatenate(idxs, axis=0)


_PAIR_RANGES = ((0, 24), (24, 29), (32, 36), (40, 43), (44, 46), (48, 50), (52, 54), (56, 64))


def _pruned_pairs(p1, p2, comb):
    lo = lax.broadcasted_iota(jnp.int32, (8, 128), 0) < 4
    p2a = p2[0:8]
    p2d = jnp.where(lo, p2a, pltpu.roll(p2a, 4, 0))
    blocks = [comb(p1[0:1], p2a), comb(p1[0:1], p2[8:16]), comb(p1[1:2], p2a), comb(p1[2:3], p2a),
              comb(p1[3:4], p2a), comb(jnp.where(lo, p1[4:5], p1[5:6]), p2d),
              comb(jnp.where(lo, p1[6:7], p1[7:8]), p2d), comb(p1[8:16], p2[0:1])]
    return jnp.concatenate(blocks, axis=0)


def _peer_topk_kernel(q_ref, keys_ref, idx_ref, gw_ref):
    for h in range(PEER_TOPK_HEADS):
        q = q_ref[:, 2 * PEER_NKEYS * h:2 * PEER_NKEYS * (h + 1)].astype(BF16)
        s1 = _dot_nt(keys_ref[h, 0].astype(BF16), q[:, :PEER_NKEYS])
        s2 = _dot_nt(keys_ref[h, 1].astype(BF16), q[:, PEER_NKEYS:])
        v1, i1 = _topk_rows(s1, PEER_TOPK)
        v2, i2 = _topk_rows(s2, PEER_TOPK)
        cand = _pruned_pairs(v1, v2, lambda a, b: a + b)
        cidx = _pruned_pairs(i1, i2, lambda a, b: a * float(PEER_NKEYS) + b)
        row = lax.broadcasted_iota(jnp.int32, cand.shape, 0)
        valid = functools.reduce(jnp.logical_or, [jnp.logical_and(row >= lo, row < hi) for lo, hi in _PAIR_RANGES])
        top_s, top_i = _topk_rows(jnp.where(valid, cand, -jnp.inf), PEER_TOPK, payload=cidx)
        e = jnp.exp(top_s - top_s[0:1, :])
        rows = slice(h * PEER_TOPK, (h + 1) * PEER_TOPK)
        gw_ref[0, rows, :] = e / jnp.sum(e, axis=0, keepdims=True)
        idx_ref[0, rows, :] = (top_i * 4.0).astype(jnp.int32)


def _peer_topk(q, keys):
    t = q.shape[0]
    nblk = t // 128
    hs = PEER_TOPK_HEADS
    shape = jax.ShapeDtypeStruct((nblk, PEER_PICKS, 128), jnp.int32)
    oblk = pl.BlockSpec((1, hs * PEER_TOPK, 128), lambda i, h: (i, h, 0))
    return pl.pallas_call(
        _peer_topk_kernel,
        grid=(nblk, PEER_HEADS // hs),
        in_specs=[pl.BlockSpec((128, hs * 2 * PEER_NKEYS), lambda i, h: (i, h)),
                  pl.BlockSpec((hs, 2, PEER_NKEYS, PEER_NKEYS), lambda i, h: (h, 0, 0, 0))],
        out_specs=[oblk, oblk],
        out_shape=[shape, jax.ShapeDtypeStruct(shape.shape, F32)],
        compiler_params=_params(("parallel", "parallel")),
        name="peer_topk",
    )(q, keys)


def _table_spec(table):
    return pl.BlockSpec(table.shape, lambda i: (0, 0), pipeline_mode=pl.Buffered(1))


def _own_row_mask():
    lane = lax.broadcasted_iota(jnp.int32, (8, 8 * PEER_PICKS), 1)
    sub = lax.broadcasted_iota(jnp.int32, (8, 8 * PEER_PICKS), 0)
    return (lane % 8) == (2 * (sub % 4) + sub // 4)


def _unpack(words):
    lo = lax.bitcast_convert_type(lax.shift_left(words, jnp.uint32(16)), F32)
    hi = lax.bitcast_convert_type(jnp.bitwise_and(words, jnp.uint32(0xFFFF0000)), F32)
    return lo, hi


def _peer_act_kernel(idx_ref, x_ref, gw_ref, tab_ref, o_ref, prod_ref):
    rows_per_tok = 4 * PEER_PICKS

    def gather(t, _):
        xlo = x_ref[t, 0:4, :]
        xhi = x_ref[t, 4:8, :]
        base = pl.multiple_of(t * rows_per_tok, rows_per_tok)
        for k in range(PEER_PICKS):
            r = pl.multiple_of(idx_ref[t, k], 4)
            lo, hi = _unpack(tab_ref[pl.ds(r, 4), :])
            prod_ref[pl.ds(base + 4 * k, 4), :] = lo * xlo + hi * xhi
        return 0

    lax.fori_loop(0, PEER_TB, gather, 0)

    ones = jnp.ones((128, 128), BF16)
    own = (lax.broadcasted_iota(jnp.int32, (rows_per_tok, PEER_PICKS), 0) // 4
           == lax.broadcasted_iota(jnp.int32, (rows_per_tok, PEER_PICKS), 1))
    own = jnp.where(own, 1.0, 0.0)

    def tail(g, _):
        acts = []
        for tt in range(8):
            base = pl.multiple_of((g * 8 + tt) * rows_per_tok, rows_per_tok)
            sums = _dot(prod_ref[pl.ds(base, rows_per_tok), :].astype(BF16), ones)
            acts.append(jnp.sum(sums * own, axis=0, keepdims=True))
        r0 = pl.multiple_of(g * 8, 8)
        o_ref[pl.ds(r0, 8), :] = _gelu(jnp.concatenate(acts, axis=0)) * gw_ref[pl.ds(r0, 8), :]
        return 0

    for g in range(PEER_TB // 8):
        tail(g, 0)


def _peer_act(idx, x8, gw, table):
    t = idx.shape[0]
    return pl.pallas_call(
        _peer_act_kernel,
        grid=(t // PEER_TB,),
        in_specs=[pl.BlockSpec((PEER_TB, PEER_PICKS), lambda i: (i, 0), memory_space=pltpu.SMEM),
                  pl.BlockSpec((PEER_TB, 8, 128), lambda i: (i, 0, 0)),
                  pl.BlockSpec((PEER_TB, PEER_PICKS), lambda i: (i, 0)),
                  _table_spec(table)],
        out_specs=pl.BlockSpec((PEER_TB, PEER_PICKS), lambda i: (i, 0)),
        out_shape=jax.ShapeDtypeStruct((t, PEER_PICKS), F32),
        scratch_shapes=[pltpu.VMEM((PEER_TB * 4 * PEER_PICKS, 128), F32)],
        compiler_params=_params(("arbitrary",)),
        name="peer_act",
    )(idx, x8, gw, table)


def _gather_rows(idx_ref, tab_ref, rows_ref):
    n = 4 * PEER_PICKS

    def gather(t, _):
        base = pl.multiple_of(t * n, n)
        for k in range(PEER_PICKS):
            r = pl.multiple_of(idx_ref[t, k], 4)
            rows_ref[pl.ds(base + 4 * k, 4), :] = tab_ref[pl.ds(r, 4), :]
        return 0

    lax.fori_loop(0, PEER_TB, gather, 0)


def _token_rows(rows_ref, t):
    n = 4 * PEER_PICKS
    return pltpu.bitcast(rows_ref[pl.ds(pl.multiple_of(t * n, n), n), :], jnp.bfloat16)


def _peer_out_kernel(idx_ref, w_ref, x_ref, tab_ref, o_ref, rows_ref, wrep_ref):
    _gather_rows(idx_ref, tab_ref, rows_ref)
    n = 8 * PEER_PICKS
    rep = (lax.broadcasted_iota(jnp.int32, (PEER_PICKS, n), 1) // 8
           == lax.broadcasted_iota(jnp.int32, (PEER_PICKS, n), 0))
    wrep_ref[...] = _dot(w_ref[...].astype(BF16), jnp.where(rep, 1.0, 0.0).astype(BF16))
    own = _own_row_mask()

    def tail(g, _):
        for tt in range(4):
            t = g * 4 + tt
            sel = jnp.where(own, jnp.broadcast_to(wrep_ref[pl.ds(t, 1), :], (8, n)), 0.0)
            o_ref[t] = x_ref[t] + _dot(sel.astype(jnp.bfloat16), _token_rows(rows_ref, t))
        return 0

    for g in range(PEER_TB // 4):
        tail(g, 0)


def _peer_out(idx, w, x8, table):
    t = idx.shape[0]
    return pl.pallas_call(
        _peer_out_kernel,
        grid=(t // PEER_TB,),
        in_specs=[pl.BlockSpec((PEER_TB, PEER_PICKS), lambda i: (i, 0), memory_space=pltpu.SMEM),
                  pl.BlockSpec((PEER_TB, PEER_PICKS), lambda i: (i, 0)),
                  pl.BlockSpec((PEER_TB, 8, 128), lambda i: (i, 0, 0)),
                  _table_spec(table)],
        out_specs=pl.BlockSpec((PEER_TB, 8, 128), lambda i: (i, 0, 0)),
        out_shape=jax.ShapeDtypeStruct((t, 8, 128), F32),
        scratch_shapes=[pltpu.VMEM((PEER_TB * 4 * PEER_PICKS, 128), jnp.uint32),
                        pltpu.VMEM((PEER_TB, 8 * PEER_PICKS), F32)],
        compiler_params=_params(("arbitrary",)),
        name="peer_out",
    )(idx, w, x8, table)


def _pack_table(tab):
    e, d = tab.shape
    bits = lax.bitcast_convert_type(tab.astype(jnp.bfloat16), jnp.uint16).astype(jnp.uint32)
    words = bits[:, :d // 2] | (bits[:, d // 2:] << 16)
    return words.reshape(e * 4, 128)


def _norm_kernel(x_ref, w_ref, o_ref):
    x = x_ref[...]
    o_ref[...] = x * lax.rsqrt(jnp.mean(x * x, axis=-1, keepdims=True) + NORM_EPS) * w_ref[...]


def _final_norm(x2, w):
    t, d = x2.shape
    tm = 512
    return pl.pallas_call(
        _norm_kernel,
        grid=(t // tm,),
        in_specs=[pl.BlockSpec((tm, d), lambda i: (i, 0)), pl.BlockSpec((1, d), lambda i: (0, 0))],
        out_specs=pl.BlockSpec((tm, d), lambda i: (i, 0)),
        out_shape=jax.ShapeDtypeStruct((t, d), F32),
        compiler_params=_params(("parallel",)),
        name="final_norm",
    )(x2, w)


def _permute_w_in(w):
    cols = [w[:, :2816],
            w[:, 2820:3076],
            w[:, 3076:3140], w[:, 3140:3204],
            w[:, 3204:3268], w[:, 3332:3396],
            w[:, 3268:3332], w[:, 3396:3460],
            w[:, 2816:2820], w[:, 3460:3472],
            jnp.zeros((w.shape[0], W_SMALL - 16), w.dtype)]
    return jnp.concatenate(cols, axis=1).astype(BF16)


def _pad_lanes(v, n=128):
    return jnp.pad(v, (0, n - v.shape[0]))[None, :]


def kernel(x, norm_mix, w_in, ret_norm, ssm_conv_w, ssm_conv_b, ssm_dt_bias, ssm_a_log, ssm_d, ssm_norm, nsa_pe_k, nsa_w1_k, nsa_w2_k, nsa_pe_v, nsa_w1_v, nsa_w2_v, w_out, norm_ffn, peer_wq, peer_keys, peer_u, peer_v, norm_final):
    b, s, d = x.shape
    t = b * s
    depth = w_in.shape[0]
    rc, ra, rb = (jnp.asarray(a) for a in _rope_tables())
    intra, qd, kd, cd = (jnp.asarray(a) for a in _ret_tables())
    ovlt = jnp.asarray(_nsa_overlap().T).astype(BF16)
    tril = jnp.asarray(np.tril(np.ones((SSM_CHUNK, SSM_CHUNK), np.float32))).astype(BF16)

    for i in range(depth):
        y_ret, y_moba, y_ssm, y_nsa, y_small = _inproj(
            x, norm_mix[i][None, :], _permute_w_in(w_in[i]), rc, ra, rb)
        o_ret = _retention(y_ret, intra, qd, kd, cd, ret_norm[i][None, :])
        o_moba = _moba(y_moba)
        o_ssm = _ssd(y_ssm, y_small, ssm_conv_w[i], ssm_conv_b[i][None, :],
                     _pad_lanes(jnp.pad(ssm_dt_bias[i], (0, 0))), _pad_lanes(ssm_a_log[i]),
                     jnp.repeat(ssm_d[i], HEAD_DIM)[None, :], ssm_norm[i][None, :], tril)
        k16 = y_nsa[:, :, 256:320].reshape(b, s // NSA_CMP_STRIDE, NSA_CMP_STRIDE * HEAD_DIM)
        v16 = y_nsa[:, :, 320:384].reshape(b, s // NSA_CMP_STRIDE, NSA_CMP_STRIDE * HEAD_DIM)
        kc, vc = _nsa_compress(k16, v16,
                               nsa_pe_k[i].reshape(1, -1), nsa_w1_k[i].astype(BF16), nsa_w2_k[i].astype(BF16),
                               nsa_pe_v[i].reshape(1, -1), nsa_w1_v[i].astype(BF16), nsa_w2_v[i].astype(BF16))
        o_nsa = _nsa_attend(y_nsa, y_small, kc, vc, ovlt)
        mixed = [o.reshape(t, GROUP_WIDTH) for o in (o_ret, o_moba, o_ssm, o_nsa)]
        x_mid, hn, q = _outproj(x.reshape(t, d), mixed, w_out[i].astype(BF16), norm_ffn[i][None, :],
                                peer_wq[i].astype(BF16))
        idx_t, gw_t = _peer_topk(q, peer_keys[i])
        idx = idx_t.transpose(0, 2, 1).reshape(t, PEER_PICKS)
        gw = gw_t.transpose(0, 2, 1).reshape(t, PEER_PICKS)
        wts = _peer_act(idx, hn.reshape(t, 8, 128), gw, _pack_table(peer_u[i]))
        x = _peer_out(idx, wts, x_mid.reshape(t, 8, 128), _pack_table(peer_v[i])).reshape(b, s, d)
    return _final_norm(x.reshape(t, d), norm_final[None, :]).reshape(b, s, d)
```
